```python
import math
import jax, jax.numpy as jnp
from jax import lax
import numpy as np

D_MODEL = 1024
BATCH = 4
SEQ = 8192
DEPTH = 2

GRID_W = 64
CTX_LEN = 256
RMS_EPS = 1e-6

D_INNER = D_MODEL
SSM_HEAD_DIM = 64
SSM_HEADS = D_INNER // SSM_HEAD_DIM
N_GROUPS = 4
HEADS_PER_GROUP = SSM_HEADS // N_GROUPS
D_STATE = 128
D_CONV = 5
CONV_PAD = D_CONV // 2
CHUNK = 128
CONV_DIM = D_INNER + 2 * N_GROUPS * D_STATE
DT_MIN = 1e-3
DT_MAX = 1e-1

HEAD_DIM = 64
V_HEAD_DIM = 2 * HEAD_DIM
ATTN_HEADS = D_MODEL // V_HEAD_DIM
ATTN_SCALE = HEAD_DIM ** -0.5
Q_BLOCK = 128
ROPE_BASE = 10000.0
ROPE_AXIS_DIM = HEAD_DIM // 2
ROPE_AXIS_PAIRS = ROPE_AXIS_DIM // 2

FOURIER_W = D_MODEL
FOURIER_GROUPS = 4
FOURIER_GROUP_W = FOURIER_W // FOURIER_GROUPS

BRANCH_W = D_MODEL
N_BRANCHES = 3

XBC_W = CONV_DIM
DT_W = 2 * SSM_HEADS
QK_W = ATTN_HEADS * 2 * HEAD_DIM
V_W = ATTN_HEADS * V_HEAD_DIM
CTX_SIDE_W = XBC_W + DT_W + QK_W + V_W
SPLIT_POINTS = (XBC_W, XBC_W + DT_W, XBC_W + DT_W + QK_W, CTX_SIDE_W,
                CTX_SIDE_W + D_INNER, CTX_SIDE_W + D_INNER + QK_W,
                CTX_SIDE_W + D_INNER + QK_W + FOURIER_W)
N_IN = CTX_SIDE_W + D_INNER + QK_W + FOURIER_W + N_BRANCHES * D_MODEL

N_EXPERTS = 32
TOP_K = 4
D_EXPERT = D_MODEL
SWIGLU_LIMIT = 7.0
SWIGLU_ALPHA = 1.702
EXPERT_BLOCK = 128

kernel_name = "hybrid_ssd_diffattn_fourier_moe_dit"

F32 = jnp.float32


def rmsnorm(x, w):
    xf = x.astype(F32)
    y = xf * lax.rsqrt(jnp.mean(xf * xf, axis=-1, keepdims=True) + RMS_EPS)
    return (y * w.astype(F32)).astype(x.dtype)


def modulate(h, shift, scale):
    return h * (1 + scale) + shift


def dwconv_centred(u, w, b):
    n_ch = u.shape[-1]
    kern = w.T[:, None, :].astype(u.dtype)
    y = lax.conv_general_dilated(u, kern, window_strides=(1,), padding=((CONV_PAD, CONV_PAD),),
                                 dimension_numbers=('NWC', 'WIO', 'NWC'), feature_group_count=n_ch)
    return y + b


def axial_rope_tables(n_rows, dtype):
    t = jnp.arange(n_rows * GRID_W)
    row = (t // GRID_W).astype(F32)
    col = (t % GRID_W).astype(F32)
    inv_freq = ROPE_BASE ** (-jnp.arange(ROPE_AXIS_PAIRS, dtype=F32) / ROPE_AXIS_PAIRS)
    ang_r = row[:, None] * inv_freq
    ang_c = col[:, None] * inv_freq
    ang = jnp.concatenate([ang_r, ang_r, ang_c, ang_c], axis=-1)
    return jnp.cos(ang).astype(dtype), jnp.sin(ang).astype(dtype)


def apply_axial_rope(t, cos, sin):
    r = t.reshape(*t.shape[:-1], 2, 2, ROPE_AXIS_PAIRS)
    rot = jnp.concatenate([-r[..., 1:2, :], r[..., 0:1, :]], axis=-2).reshape(t.shape)
    return t * cos[None, :, None, None, :] + rot * sin[None, :, None, None, :]


def ssd_scan(xs, dt, a, bm, cm, h0, with_output):
    bsz, seqlen, n_heads, hd = xs.shape
    n_groups, d_state = bm.shape[2], bm.shape[3]
    hg = n_heads // n_groups
    nc = seqlen // CHUNK
    xc = xs.astype(F32).reshape(bsz, nc, CHUNK, n_groups, hg, hd)
    dtc = dt.reshape(bsz, nc, CHUNK, n_groups, hg)
    bc = bm.astype(F32).reshape(bsz, nc, CHUNK, n_groups, d_state)
    cc = cm.astype(F32).reshape(bsz, nc, CHUNK, n_groups, d_state)
    a_cs = jnp.cumsum(dtc * a.reshape(n_groups, hg), axis=2)
    a_last = a_cs[:, :, -1]
    w_end = jnp.exp(a_last[:, :, None] - a_cs) * dtc
    states = jnp.einsum('bcjgn,bcjgh,bcjghp->bcghpn', bc, w_end, xc)

    def step(h, inp):
        st, dec = inp
        return h * dec[..., None, None] + st, h

    h_final, h_prev = lax.scan(step, h0, (jnp.moveaxis(states, 1, 0), jnp.moveaxis(jnp.exp(a_last), 1, 0)))
    if not with_output:
        return None, h_final
    h_prev = jnp.moveaxis(h_prev, 0, 1)
    lower = jnp.tril(jnp.ones((CHUNK, CHUNK), dtype=bool))
    seg = a_cs[:, :, :, None] - a_cs[:, :, None]
    decay = jnp.exp(jnp.where(lower[:, :, None, None], seg, -jnp.inf))
    cb = jnp.einsum('bcign,bcjgn->bcijg', cc, bc)
    w_ij = cb[..., None] * decay * dtc[:, :, None]
    y = jnp.einsum('bcijgh,bcjghp->bcighp', w_ij, xc)
    y = y + jnp.einsum('bcign,bcghpn->bcighp', cc, h_prev) * jnp.exp(a_cs)[..., None]
    return y.reshape(bsz, seqlen, n_heads, hd).astype(xs.dtype), h_final


def bidir_ssd(xs, dt, a, bm, cm, h0_f, h0_b, with_output):
    rev = lambda t: jnp.flip(t, axis=1)
    y_f, h_f = ssd_scan(xs, dt[:, :, 0], a[0], bm, cm, h0_f, with_output)
    y_b, h_b = ssd_scan(rev(xs), rev(dt[:, :, 1]), a[1], rev(bm), rev(cm), h0_b, with_output)
    if not with_output:
        return None, h_f, h_b
    return y_f + rev(y_b), h_f, h_b


def ssd_prepare(xbc, dt_raw, conv_w, conv_b, dt_bias):
    bsz, seqlen, _ = xbc.shape
    xbc = jax.nn.silu(dwconv_centred(xbc, conv_w, conv_b))
    xs, bm, cm = jnp.split(xbc, [D_INNER, D_INNER + N_GROUPS * D_STATE], axis=-1)
    xs = xs.reshape(bsz, seqlen, SSM_HEADS, SSM_HEAD_DIM)
    bm = bm.reshape(bsz, seqlen, N_GROUPS, D_STATE)
    cm = cm.reshape(bsz, seqlen, N_GROUPS, D_STATE)
    dt = jax.nn.softplus((dt_raw.reshape(bsz, seqlen, 2, SSM_HEADS) + dt_bias).astype(F32))
    return xs, dt, bm, cm


def ssd_output(y_bi, xs, z, d_skip, ssm_norm_w):
    bsz, seqlen = xs.shape[:2]
    y = (y_bi + d_skip[:, None] * xs).reshape(bsz, seqlen, D_INNER)
    return rmsnorm(y * jax.nn.silu(z), ssm_norm_w)


def diff_attend(q, k, v, lam):
    s = jnp.einsum('bqhmd,bkhmd->bhmqk', q, k).astype(F32) * ATTN_SCALE
    p = jax.nn.softmax(s, axis=-1)
    a = p[:, :, 0] - lam * p[:, :, 1]
    return jnp.einsum('bhqk,bkhe->bqhe', a.astype(v.dtype), v)


def diff_head_out(o, subln_w, lambda_init):
    bsz, n_tok = o.shape[:2]
    return (rmsnorm(o, subln_w) * (1.0 - lambda_init)).reshape(bsz, n_tok, ATTN_HEADS * V_HEAD_DIM)


def fourier_mix(f):
    bsz, seqlen, _ = f.shape
    fg = f.reshape(bsz, seqlen, FOURIER_GROUPS, FOURIER_GROUP_W).astype(F32)
    out = jnp.fft.fft2(fg, axes=(1, 3), norm="ortho").real
    return out.reshape(bsz, seqlen, FOURIER_W).astype(f.dtype)


def merge_branches(y_ssm, y_attn, y_four, gates, w_branch, w_out):
    g = jax.nn.sigmoid(gates.reshape(*gates.shape[:-1], N_BRANCHES, D_MODEL))
    m = (g[..., 0, :] * (y_ssm @ w_branch[0])
         + g[..., 1, :] * (y_attn @ w_branch[1])
         + g[..., 2, :] * (y_four @ w_branch[2]))
    return m @ w_out


def token_mixer(h, hc, rope_cos, rope_sin, w_in, conv_w, conv_b, dt_bias, a_log, d_skip,
                ssm_norm_w, lambda_qk, subln_w, w_branch, w_out, lambda_init, ctx_out):
    bsz, seqlen, _ = h.shape
    ctx_len = hc.shape[1]
    xbc, dt_raw, k, v, z, q, f, gates = jnp.split(h @ w_in, SPLIT_POINTS, axis=-1)
    if ctx_out:
        pcs = jnp.split(hc @ w_in, SPLIT_POINTS, axis=-1)
    else:
        pcs = jnp.split(hc @ w_in[:, :CTX_SIDE_W], SPLIT_POINTS[:3], axis=-1)
    xbc_c, dt_c, k_c, v_c = pcs[0], pcs[1], pcs[2], pcs[3]

    a = -jnp.exp(a_log.astype(F32))
    xs_c, dtv_c, bm_c, cm_c = ssd_prepare(xbc_c, dt_c, conv_w, conv_b, dt_bias)
    h0 = jnp.zeros((bsz, N_GROUPS, HEADS_PER_GROUP, SSM_HEAD_DIM, D_STATE), F32)
    yc_bi, h_f, h_b = bidir_ssd(xs_c, dtv_c, a, bm_c, cm_c, h0, h0, ctx_out)
    xs, dtv, bm, cm = ssd_prepare(xbc, dt_raw, conv_w, conv_b, dt_bias)
    y_bi, _, _ = bidir_ssd(xs, dtv, a, bm, cm, h_f, h_b, True)
    y_ssm = ssd_output(y_bi, xs, z, d_skip, ssm_norm_w)

    lq = lambda_qk.astype(F32)
    lam = jnp.exp(jnp.sum(lq[0] * lq[1])) - jnp.exp(jnp.sum(lq[2] * lq[3])) + lambda_init
    q = apply_axial_rope(q.reshape(bsz, seqlen, ATTN_HEADS, 2, HEAD_DIM), rope_cos, rope_sin)
    k = apply_axial_rope(k.reshape(bsz, seqlen, ATTN_HEADS, 2, HEAD_DIM), rope_cos, rope_sin)
    k_c = k_c.reshape(bsz, ctx_len, ATTN_HEADS, 2, HEAD_DIM)
    v_c = v_c.reshape(bsz, ctx_len, ATTN_HEADS, V_HEAD_DIM)
    k_all = jnp.concatenate([k, k_c], axis=1)
    v_all = jnp.concatenate([v.reshape(bsz, seqlen, ATTN_HEADS, V_HEAD_DIM), v_c], axis=1)
    n_blk = seqlen // Q_BLOCK
    qb = jnp.moveaxis(q.reshape(bsz, n_blk, Q_BLOCK, ATTN_HEADS, 2, HEAD_DIM), 1, 0)
    o = lax.map(lambda qi: diff_attend(qi, k_all, v_all, lam), qb)
    o = jnp.moveaxis(o, 0, 1).reshape(bsz, seqlen, ATTN_HEADS, V_HEAD_DIM)
    y_attn = diff_head_out(o, subln_w, lambda_init)

    y_four = fourier_mix(f)

    out = merge_branches(y_ssm, y_attn, y_four, gates, w_branch, w_out)
    if not ctx_out:
        return out, None

    z_c, q_c, f_c, gates_c = pcs[4], pcs[5], pcs[6], pcs[7]
    y_ssm_c = ssd_output(yc_bi, xs_c, z_c, d_skip, ssm_norm_w)
    o_c = diff_attend(q_c.reshape(bsz, ctx_len, ATTN_HEADS, 2, HEAD_DIM), k_c, v_c, lam)
    y_attn_c = diff_head_out(o_c, subln_w, lambda_init)
    y_four_c = fourier_mix(f_c)
    out_c = merge_branches(y_ssm_c, y_attn_c, y_four_c, gates_c, w_branch, w_out)
    return out, out_c


def moe(h, router_w, router_b, w1, b1, w2, b2):
    n_tok, d = h.shape
    logits = (h @ router_w + router_b).astype(F32)
    top_val, top_idx = lax.top_k(logits, TOP_K)
    gate = jax.nn.softmax(top_val, axis=-1).astype(h.dtype)
    n_assign = n_tok * TOP_K
    flat_e = top_idx.reshape(-1)
    flat_tok = jnp.arange(n_assign, dtype=jnp.int32) // TOP_K
    flat_g = gate.reshape(-1)
    order = jnp.argsort(flat_e)
    sorted_e = flat_e[order]
    counts = jnp.bincount(flat_e, length=N_EXPERTS)
    padded = (counts + EXPERT_BLOCK - 1) // EXPERT_BLOCK * EXPERT_BLOCK
    start_sorted = jnp.cumsum(counts) - counts
    end_padded = jnp.cumsum(padded)
    start_padded = end_padded - padded
    rank = jnp.arange(n_assign, dtype=jnp.int32) - start_sorted[sorted_e]
    dest = start_padded[sorted_e] + rank
    n_slots = (n_assign + EXPERT_BLOCK - 1) // EXPERT_BLOCK * EXPERT_BLOCK + N_EXPERTS * EXPERT_BLOCK
    n_blocks = n_slots // EXPERT_BLOCK
    slot_tok = jnp.full((n_slots,), n_tok, jnp.int32).at[dest].set(flat_tok[order])
    slot_gate = jnp.zeros((n_slots,), h.dtype).at[dest].set(flat_g[order])
    block_e = jnp.minimum(jnp.searchsorted(end_padded, jnp.arange(n_blocks) * EXPERT_BLOCK, side='right'),
                          N_EXPERTS - 1)
    h_pad = jnp.concatenate([h, jnp.zeros((1, d), h.dtype)], axis=0)
    xs = h_pad[slot_tok].reshape(n_blocks, EXPERT_BLOCK, d)

    def expert_block(args):
        xb, e = args
        gu = xb @ w1[e] + b1[e]
        g, u = gu[:, :D_EXPERT], gu[:, D_EXPERT:]
        g = jnp.minimum(g, SWIGLU_LIMIT)
        u = jnp.clip(u, -SWIGLU_LIMIT, SWIGLU_LIMIT)
        act = g * jax.nn.sigmoid(SWIGLU_ALPHA * g) * (u + 1)
        return act @ w2[e] + b2[e]

    ys = lax.map(expert_block, (xs, block_e)).reshape(n_slots, d)
    out = jnp.zeros((n_tok + 1, d), h.dtype).at[slot_tok].add(ys * slot_gate[:, None])
    return out[:n_tok]


def setup_inputs(seed: int = 0) -> dict:
    key = jax.random.key(seed)
    ks = jax.random.split(key, 24)
    L = DEPTH

    def nrm(k, shape, scale):
        return jax.random.normal(k, shape, F32) * scale

    dt0 = jnp.exp(jax.random.uniform(ks[10], (L, 2, SSM_HEADS), F32,
                                     minval=math.log(DT_MIN), maxval=math.log(DT_MAX)))
    return {
        "x": nrm(ks[0], (BATCH, SEQ, D_MODEL), 1.0),
        "c": nrm(ks[1], (BATCH, D_MODEL), 1.0),
        "ctx": nrm(ks[2], (BATCH, CTX_LEN, D_MODEL), 1.0),
        "c_ctx": nrm(ks[3], (D_MODEL,), 1.0),
        "ada_w": nrm(ks[4], (L, D_MODEL, 6 * D_MODEL), 0.5 * D_MODEL ** -0.5),
        "ada_b": nrm(ks[5], (L, 6 * D_MODEL), 0.01),
        "norm_w": 1.0 + nrm(ks[6], (L, 4, D_MODEL), 0.05),
        "w_in": nrm(ks[7], (L, D_MODEL, N_IN), D_MODEL ** -0.5),
        "conv_w": nrm(ks[8], (L, CONV_DIM, D_CONV), D_CONV ** -0.5),
        "conv_b": nrm(ks[9], (L, CONV_DIM), 0.01),
        "dt_bias": dt0 + jnp.log(-jnp.expm1(-dt0)),
        "a_log": jnp.log(jax.random.uniform(ks[11], (L, 2, SSM_HEADS), F32, minval=1.0, maxval=16.0)),
        "d_skip": 1.0 + nrm(ks[12], (L, SSM_HEADS), 0.1),
        "ssm_norm_w": 1.0 + nrm(ks[13], (L, D_INNER), 0.05),
        "lambda_qk": nrm(ks[14], (L, 4, HEAD_DIM), 0.1),
        "subln_w": 1.0 + nrm(ks[15], (L, V_HEAD_DIM), 0.05),
        "w_branch": nrm(ks[16], (L, N_BRANCHES, BRANCH_W, D_MODEL), BRANCH_W ** -0.5),
        "w_out": nrm(ks[17], (L, D_MODEL, D_MODEL), D_MODEL ** -0.5),
        "router_w": nrm(ks[18], (L, D_MODEL, N_EXPERTS), D_MODEL ** -0.5),
        "router_b": nrm(ks[19], (L, N_EXPERTS), 0.01),
        "expert_w1": nrm(ks[20], (L, N_EXPERTS, D_MODEL, 2 * D_EXPERT), D_MODEL ** -0.5),
        "expert_b1": nrm(ks[21], (L, N_EXPERTS, 2 * D_EXPERT), 0.01),
        "expert_w2": nrm(ks[22], (L, N_EXPERTS, D_EXPERT, D_MODEL), D_EXPERT ** -0.5),
        "expert_b2": nrm(ks[23], (L, N_EXPERTS, D_MODEL), 0.01),
    }


def reference(x, c, ctx, c_ctx, ada_w, ada_b, norm_w, w_in, conv_w, conv_b, dt_bias, a_log,
              d_skip, ssm_norm_w, lambda_qk, subln_w, w_branch, w_out, router_w, router_b,
              expert_w1, expert_b1, expert_w2, expert_b2):
    bsz, seqlen, _ = x.shape
    n_rows = seqlen // GRID_W
    rope_cos, rope_sin = axial_rope_tables(n_rows, x.dtype)
    for layer in range(DEPTH):
        last = layer == DEPTH - 1
        lambda_init = 0.8 - 0.6 * math.exp(-0.3 * layer)
        mod = jax.nn.silu(c) @ ada_w[layer] + ada_b[layer]
        mod_c = jax.nn.silu(c_ctx) @ ada_w[layer] + ada_b[layer]
        sh1, sc1, g1, sh2, sc2, g2 = jnp.split(mod[:, None, :], 6, axis=-1)
        csh1, csc1, cg1, csh2, csc2, cg2 = jnp.split(mod_c, 6)
        nw = norm_w[layer]

        h = modulate(rmsnorm(x, nw[0]), sh1, sc1)
        hc = modulate(rmsnorm(ctx, nw[0]), csh1, csc1)
        mix, mix_c = token_mixer(h, hc, rope_cos, rope_sin, w_in[layer], conv_w[layer], conv_b[layer],
                                 dt_bias[layer], a_log[layer], d_skip[layer], ssm_norm_w[layer],
                                 lambda_qk[layer], subln_w[layer], w_branch[layer], w_out[layer],
                                 lambda_init, not last)
        x = x + g1 * rmsnorm(mix, nw[1])

        h = modulate(rmsnorm(x, nw[2]), sh2, sc2).reshape(-1, D_MODEL)
        moe_args = (router_w[layer], router_b[layer], expert_w1[layer], expert_b1[layer],
                    expert_w2[layer], expert_b2[layer])
        if last:
            y = moe(h, *moe_args)
            x = x + g2 * rmsnorm(y.reshape(x.shape), nw[3])
        else:
            ctx = ctx + cg1 * rmsnorm(mix_c, nw[1])
            hc = modulate(rmsnorm(ctx, nw[2]), csh2, csc2).reshape(-1, D_MODEL)
            y = moe(jnp.concatenate([h, hc], axis=0), *moe_args)
            n_lat = h.shape[0]
            x = x + g2 * rmsnorm(y[:n_lat].reshape(x.shape), nw[3])
            ctx = ctx + cg2 * rmsnorm(y[n_lat:].reshape(ctx.shape), nw[3])
    return x
```

```python
import functools
import math

import numpy as np
import jax
import jax.numpy as jnp
from jax import lax
from jax.experimental import pallas as pl
from jax.experimental.pallas import tpu as pltpu

F32 = jnp.float32
BF16 = jnp.bfloat16
HI = lax.Precision.HIGHEST

D = 1024
GRID_W = 64
EPS = 1e-6
SSM_HEADS = 16
SSM_P = 64
N_GROUPS = 4
D_STATE = 128
CHUNK = 128
XBC_W = D + 2 * N_GROUPS * D_STATE
DT_W = 2 * SSM_HEADS
HEAD_DIM = 64
ATTN_HEADS = 8
ATTN_SCALE = HEAD_DIM ** -0.5
ROPE_BASE = 10000.0
ROPE_PAIRS = 16
FOURIER_GROUPS = 4
FOURIER_GW = 256
N_EXPERTS = 32
TOP_K = 4
SWIGLU_LIMIT = 7.0
SWIGLU_ALPHA = 1.702
LOG2E = 1.4426950408889634

COL_XBC, COL_K, COL_V, COL_Z, COL_Q, COL_F, COL_G = 0, 2048, 3072, 4096, 5120, 6144, 7168
NP = 10240

VMEM_LIMIT = 52 * 1024 * 1024
MOE_BLOCK = 256


def _cp(*sem):
    return pltpu.CompilerParams(dimension_semantics=sem, vmem_limit_bytes=VMEM_LIMIT)


def _sigmoid(x):
    return 1.0 / (1.0 + jnp.exp(-x))


def _rms(x):
    return x * lax.rsqrt(jnp.mean(x * x, axis=-1, keepdims=True) + EPS)


def _proj_kernel(x_ref, mul_ref, add_ref, w_ref, wdt_ref, o_ref, dt_ref, h_scr):
    @pl.when(pl.program_id(2) == 0)
    def _():
        h = _rms(x_ref[0]) * mul_ref[0] + add_ref[0]
        hb = h.astype(BF16)
        h_scr[...] = hb
        dt_ref[0] = jnp.dot(hb, wdt_ref[...], preferred_element_type=F32)

    o_ref[0] = jnp.dot(h_scr[...], w_ref[...], preferred_element_type=F32).astype(BF16)


def _project(x, mul, add, w, wdt, tm):
    b, l, _ = x.shape
    tn = 1024
    return pl.pallas_call(
        _proj_kernel,
        grid=(b, l // tm, NP // tn),
        in_specs=[
            pl.BlockSpec((1, tm, D), lambda bi, i, j: (bi, i, 0)),
            pl.BlockSpec((1, 1, D), lambda bi, i, j: (bi, 0, 0)),
            pl.BlockSpec((1, 1, D), lambda bi, i, j: (bi, 0, 0)),
            pl.BlockSpec((D, tn), lambda bi, i, j: (0, j)),
            pl.BlockSpec((D, 128), lambda bi, i, j: (0, 0)),
        ],
        out_specs=[
            pl.BlockSpec((1, tm, tn), lambda bi, i, j: (bi, i, j)),
            pl.BlockSpec((1, tm, 128), lambda bi, i, j: (bi, i, 0)),
        ],
        out_shape=[jax.ShapeDtypeStruct((b, l, NP), BF16), jax.ShapeDtypeStruct((b, l, 128), F32)],
        scratch_shapes=[pltpu.VMEM((tm, D), BF16)],
        compiler_params=_cp("parallel", "parallel", "arbitrary"),
    )(x, mul, add, w, wdt)


def _conv_kernel(x_ref, prev_ref, next_ref, w_ref, b_ref, o_ref, ext_scr):
    i = pl.program_id(1)
    n = pl.num_programs(1)
    tr = x_ref.shape[1]
    prev = prev_ref[0].astype(F32)[8:16]
    nxt = next_ref[0].astype(F32)[0:8]
    ext_scr[0:8, :] = jnp.where(i > 0, prev, 0.0)
    ext_scr[8:8 + tr, :] = x_ref[0].astype(F32)
    ext_scr[8 + tr:16 + tr, :] = jnp.where(i < n - 1, nxt, 0.0)
    acc = b_ref[...] + w_ref[0:1, :] * ext_scr[6:6 + tr, :]
    for k in range(1, 5):
        acc = acc + w_ref[k:k + 1, :] * ext_scr[6 + k:6 + k + tr, :]
    o_ref[0] = (acc * _sigmoid(acc)).astype(BF16)


def _conv_silu(p, conv_wt, conv_b, tr):
    b, l, _ = p.shape
    nh = tr // 16
    last = l // 16 - 1
    return pl.pallas_call(
        _conv_kernel,
        grid=(b, l // tr),
        in_specs=[
            pl.BlockSpec((1, tr, XBC_W), lambda bi, i: (bi, i, 0)),
            pl.BlockSpec((1, 16, XBC_W), lambda bi, i: (bi, jnp.maximum(i * nh - 1, 0), 0)),
            pl.BlockSpec((1, 16, XBC_W), lambda bi, i: (bi, jnp.minimum((i + 1) * nh, last), 0)),
            pl.BlockSpec((5, XBC_W), lambda bi, i: (0, 0)),
            pl.BlockSpec((1, XBC_W), lambda bi, i: (0, 0)),
        ],
        out_specs=pl.BlockSpec((1, tr, XBC_W), lambda bi, i: (bi, i, 0)),
        out_shape=jax.ShapeDtypeStruct((b, l, XBC_W), BF16),
        scratch_shapes=[pltpu.VMEM((tr + 16, XBC_W), F32)],
        compiler_params=_cp("parallel", "parallel"),
    )(p, p, p, conv_wt, conv_b)


def _softplus(x):
    return jnp.maximum(x, 0.0) + jnp.log1p(jnp.exp(-jnp.abs(x)))


def _ssd_kernel(xbc_ref, dtc_ref, dtr_ref, prow_ref, pcol_ref, e_ref, o_ref, state_scr, *, reverse):
    @pl.when(pl.program_id(1) == 0)
    def _():
        state_scr[...] = jnp.zeros_like(state_scr)

    q = CHUNK
    row = lax.broadcasted_iota(jnp.int32, (q, q), 0)
    col = lax.broadcasted_iota(jnp.int32, (q, q), 1)
    keep = (col >= row) if reverse else (col <= row)
    tri = keep.astype(F32)
    tri_t = ((row >= col) if reverse else (row <= col)).astype(F32)

    dt_col = _softplus(dtc_ref[0, 0] + prow_ref[0:1, :])
    dt_row = _softplus(dtr_ref[0, 0] + pcol_ref[:, 0:1])
    adt_col = dt_col * prow_ref[1:2, :]
    adt_row = dt_row * pcol_ref[:, 1:2]
    acs_col = jnp.dot(tri, adt_col, precision=HI, preferred_element_type=F32)
    acs_row = jnp.dot(adt_row, tri_t, precision=HI, preferred_element_type=F32)
    a_tot = acs_col[0:1, :] if reverse else acs_col[q - 1:q, :]

    expand = e_ref[...]
    wend = jnp.dot(jnp.exp(a_tot - acs_col) * dt_col, expand, precision=HI, preferred_element_type=F32)
    dec_in = jnp.dot(jnp.exp(acs_col), expand, precision=HI, preferred_element_type=F32)
    dec_state = dec_in[0:1, :] if reverse else dec_in[q - 1:q, :]

    xb = xbc_ref[0, :, 0:D]
    xw = (xb.astype(F32) * wend).astype(BF16)
    lane = lax.broadcasted_iota(jnp.int32, (q, 2 * SSM_P), 1)
    gw = (SSM_HEADS // N_GROUPS) * SSM_P
    for g in range(N_GROUPS):
        bg = xbc_ref[0, :, D + g * D_STATE:D + (g + 1) * D_STATE]
        cg = xbc_ref[0, :, D + N_GROUPS * D_STATE + g * D_STATE:D + N_GROUPS * D_STATE + (g + 1) * D_STATE]
        st = state_scr[:, g * gw:(g + 1) * gw]
        y_off = jnp.dot(cg, st.astype(BF16), preferred_element_type=F32)
        s_new = lax.dot_general(bg, xw[:, g * gw:(g + 1) * gw], (((0,), (0,)), ((), ())),
                                preferred_element_type=F32)
        state_scr[:, g * gw:(g + 1) * gw] = st * dec_state[:, g * gw:(g + 1) * gw] + s_new
        cb = lax.dot_general(cg, bg, (((1,), (1,)), ((), ())), preferred_element_type=F32)
        for pair in range(2):
            h0 = 4 * g + 2 * pair
            ws = []
            for h in (h0, h0 + 1):
                seg = acs_col[:, h:h + 1] - acs_row[h:h + 1, :]
                ws.append(jnp.exp(jnp.where(keep, seg, -jnp.inf)) * cb * dt_row[h:h + 1, :])
            lhs = jnp.concatenate(ws, axis=1).astype(BF16)
            xp = xb[:, h0 * SSM_P:(h0 + 2) * SSM_P]
            zero = jnp.zeros_like(xp)
            rhs = jnp.concatenate([jnp.where(lane < SSM_P, xp, zero), jnp.where(lane >= SSM_P, xp, zero)], axis=0)
            y_diag = jnp.dot(lhs, rhs, preferred_element_type=F32)
            c0 = h0 * SSM_P
            y = y_off[:, pair * 128:(pair + 1) * 128] * dec_in[:, c0:c0 + 128] + y_diag
            o_ref[0, :, c0:c0 + 128] = y.astype(BF16)


def _ssd_scan(xbc, dtc, dtr, prow, pcol, expand, n_lat, reverse):
    b, lt, _ = xbc.shape
    nt = lt // CHUNK
    nl = n_lat // CHUNK
    nc = nt - nl
    if reverse:
        chunk = lambda s: nt - 1 - s
    else:
        chunk = lambda s: jnp.where(s < nc, nl + s, s - nc)
    return pl.pallas_call(
        functools.partial(_ssd_kernel, reverse=reverse),
        grid=(b, nt),
        in_specs=[
            pl.BlockSpec((1, CHUNK, XBC_W), lambda bi, s: (bi, chunk(s), 0)),
            pl.BlockSpec((1, 1, CHUNK, SSM_HEADS), lambda bi, s: (bi, chunk(s), 0, 0)),
            pl.BlockSpec((1, 1, SSM_HEADS, CHUNK), lambda bi, s: (bi, chunk(s), 0, 0)),
            pl.BlockSpec((2, SSM_HEADS), lambda bi, s: (0, 0)),
            pl.BlockSpec((SSM_HEADS, 2), lambda bi, s: (0, 0)),
            pl.BlockSpec((SSM_HEADS, D), lambda bi, s: (0, 0)),
        ],
        out_specs=pl.BlockSpec((1, CHUNK, D), lambda bi, s: (bi, chunk(s), 0)),
        out_shape=jax.ShapeDtypeStruct((b, lt, D), BF16),
        scratch_shapes=[pltpu.VMEM((D_STATE, D), F32)],
        compiler_params=_cp("parallel", "arbitrary"),
    )(xbc, dtc, dtr, prow, pcol, expand)


def _rope_kernel(q_ref, k_ref, cos_ref, sin_ref, qo_ref, ko_ref):
    cos = jnp.tile(cos_ref[...], (1, D // 128))
    sin = jnp.tile(sin_ref[...], (1, D // 128))
    lane = lax.broadcasted_iota(jnp.int32, cos.shape, 1)
    first = (lane % 32) < ROPE_PAIRS

    def rot(t):
        nxt = pltpu.roll(t, D - ROPE_PAIRS, axis=1)
        prv = pltpu.roll(t, ROPE_PAIRS, axis=1)
        return t * cos + jnp.where(first, nxt, prv) * sin

    qo_ref[0] = rot(q_ref[0].astype(F32)).astype(BF16)
    ko_ref[0] = rot(k_ref[0].astype(F32)).astype(BF16)


def _rope(p, cos, sin_signed, tr):
    b, l, _ = p.shape
    spec = lambda c: pl.BlockSpec((1, tr, D), lambda bi, i: (bi, i, c))
    tab = pl.BlockSpec((tr, 128), lambda bi, i: (i, 0))
    out = pl.BlockSpec((1, tr, D), lambda bi, i: (bi, i, 0))
    return pl.pallas_call(
        _rope_kernel,
        grid=(b, l // tr),
        in_specs=[spec(COL_Q // D), spec(COL_K // D), tab, tab],
        out_specs=[out, out],
        out_shape=[jax.ShapeDtypeStruct((b, l, D), BF16)] * 2,
        compiler_params=_cp("parallel", "parallel"),
    )(p, p, cos, sin_signed)


def _attn_kernel(lam_ref, q_ref, *refs, n_src, tq, tk, scale):
    kv = refs[:2 * n_src]
    wn_ref, o_ref, qs_scr, m_scr, acc_scr = refs[2 * n_src:]
    q = q_ref[0].astype(F32) * scale
    lane = lax.broadcasted_iota(jnp.int32, q.shape, 1)
    qs_scr[0:tq, :] = jnp.where(lane < HEAD_DIM, q, 0.0).astype(BF16)
    qs_scr[tq:2 * tq, :] = jnp.where(lane >= HEAD_DIM, q, 0.0).astype(BF16)
    m_scr[...] = jnp.full_like(m_scr, -1e30)
    acc_scr[...] = jnp.zeros_like(acc_scr)

    def chunk(k_blk, v_blk):
        qs = qs_scr[...]
        s = lax.dot_general(qs, k_blk, (((1,), (1,)), ((), ())), preferred_element_type=F32)
        m_prev = m_scr[...]
        m_new = jnp.maximum(m_prev, jnp.max(s, axis=-1, keepdims=True))
        alpha = jnp.exp2(m_prev - m_new)
        p = jnp.exp2((s - m_new).astype(BF16))
        v_ext = jnp.concatenate([v_blk, jnp.ones_like(v_blk)], axis=1)
        acc_scr[...] = alpha * acc_scr[...] + jnp.dot(p, v_ext, preferred_element_type=F32)
        m_scr[...] = m_new

    for si in range(n_src):
        k_ref, v_ref = kv[2 * si], kv[2 * si + 1]
        n_keys = k_ref.shape[1]
        step = min(tk, n_keys)

        def body(j, carry, k_ref=k_ref, v_ref=v_ref, step=step):
            off = pl.multiple_of(j * step, step)
            chunk(k_ref[0, pl.ds(off, step), :], v_ref[0, pl.ds(off, step), :])
            return carry

        lax.fori_loop(0, n_keys // step, body, 0)

    acc = acc_scr[...]
    o1 = acc[0:tq, 0:128] / acc[0:tq, 128:256]
    o2 = acc[tq:2 * tq, 0:128] / acc[tq:2 * tq, 128:256]
    o = o1 - lam_ref[0] * o2
    o_ref[0] = (_rms(o) * wn_ref[...]).astype(BF16)


def _attention(lam, q, q_col, kvs, wn, tq, tk):
    b, sq = q.shape[0], q.shape[1]
    in_specs = [pl.BlockSpec(memory_space=pltpu.SMEM),
                pl.BlockSpec((1, tq, 128), lambda bi, h, i, c=q_col: (bi, i, c * ATTN_HEADS + h))]
    args = [lam, q]
    for k, kc, v, vc in kvs:
        in_specs.append(pl.BlockSpec((1, k.shape[1], 128), lambda bi, h, i, c=kc: (bi, 0, c * ATTN_HEADS + h)))
        in_specs.append(pl.BlockSpec((1, v.shape[1], 128), lambda bi, h, i, c=vc: (bi, 0, c * ATTN_HEADS + h)))
        args += [k, v]
    in_specs.append(pl.BlockSpec((1, 128), lambda bi, h, i: (0, 0)))
    args.append(wn)
    return pl.pallas_call(
        functools.partial(_attn_kernel, n_src=len(kvs), tq=tq, tk=tk, scale=ATTN_SCALE * LOG2E),
        grid=(b, ATTN_HEADS, sq // tq),
        in_specs=in_specs,
        out_specs=pl.BlockSpec((1, tq, 128), lambda bi, h, i: (bi, i, h)),
        out_shape=jax.ShapeDtypeStruct((b, sq, D), BF16),
        scratch_shapes=[pltpu.VMEM((2 * tq, 128), BF16), pltpu.VMEM((2 * tq, 1), F32),
                        pltpu.VMEM((2 * tq, 256), F32)],
        compiler_params=_cp("parallel", "parallel", "parallel"),
    )(*args)


def _dft_tables(n):
    k = np.arange(n)
    ang = 2.0 * np.pi * ((k[:, None] * k[None, :]) % n) / n
    return np.cos(ang), np.sin(ang)


def _four0_kernel(x_ref, w_ref, zr_ref, zi_ref):
    y = jnp.dot(x_ref[0], w_ref[...], preferred_element_type=F32)
    zr_ref[0] = y[:, 0:FOURIER_GW].astype(BF16)
    zi_ref[0] = y[:, FOURIER_GW:2 * FOURIER_GW].astype(BF16)


def _fourier_channel(p, tm):
    b, l, _ = p.shape
    c, s = _dft_tables(FOURIER_GW)
    w0 = jnp.asarray(np.concatenate([c, s], axis=1) / math.sqrt(FOURIER_GW), BF16)
    cb = COL_F // FOURIER_GW
    out = pl.BlockSpec((1, tm, FOURIER_GW), lambda bi, i, g: (bi, i, g))
    return pl.pallas_call(
        _four0_kernel,
        grid=(b, l // tm, FOURIER_GROUPS),
        in_specs=[pl.BlockSpec((1, tm, FOURIER_GW), lambda bi, i, g: (bi, i, cb + g)),
                  pl.BlockSpec((FOURIER_GW, 2 * FOURIER_GW), lambda bi, i, g: (0, 0))],
        out_specs=[out, out],
        out_shape=[jax.ShapeDtypeStruct((b, l, D), BF16)] * 2,
        compiler_params=_cp("parallel", "parallel", "parallel"),
    )(p, w0)


def _four1_kernel(zr_ref, zi_ref, m_ref, a_ref):
    n1 = zr_ref.shape[1]
    z = jnp.concatenate([zr_ref[0], zi_ref[0]], axis=0)
    a = jnp.dot(m_ref[0], z, preferred_element_type=F32).astype(BF16)
    a_ref[0, 0, 0] = a[0:n1]
    a_ref[0, 1, 0] = a[n1:2 * n1]


def _four2_kernel(a_ref, w_ref, o_ref):
    n2 = a_ref.shape[2]
    a = a_ref[0].reshape(2 * n2, a_ref.shape[-1])
    o_ref[0] = jnp.dot(w_ref[...], a, preferred_element_type=F32).astype(BF16)


def _fourier_seq(zr, zi):
    b, l, _ = zr.shape
    n2 = GRID_W
    n1 = l // n2
    k1 = np.arange(n1)
    ang1 = 2.0 * np.pi * ((k1[:, None] * k1[None, :]) % n1) / n1
    tw = 2.0 * np.pi * ((np.arange(n2)[:, None] * k1[None, :]) % l) / l
    ang = ang1[None, :, :] + tw[:, :, None]
    cr, si = np.cos(ang) / math.sqrt(n1), np.sin(ang) / math.sqrt(n1)
    m = jnp.asarray(np.concatenate([np.concatenate([cr, -si], axis=2),
                                    np.concatenate([si, cr], axis=2)], axis=1), BF16)
    c2, s2 = _dft_tables(n2)
    w2 = jnp.asarray(np.concatenate([c2, -s2], axis=1) / math.sqrt(n2), BF16)

    zin = pl.BlockSpec((1, n1, D), lambda bi, j: (bi, 0, j))
    a = pl.pallas_call(
        _four1_kernel,
        grid=(b, n2),
        in_specs=[zin, zin, pl.BlockSpec((1, 2 * n1, 2 * n1), lambda bi, j: (j, 0, 0))],
        out_specs=pl.BlockSpec((1, 2, 1, n1, D), lambda bi, j: (bi, 0, j, 0, 0)),
        out_shape=jax.ShapeDtypeStruct((b, 2, n2, n1, D), BF16),
        compiler_params=_cp("parallel", "parallel"),
    )(zr.reshape(b, n1, n2 * D), zi.reshape(b, n1, n2 * D), m)
    r = pl.pallas_call(
        _four2_kernel,
        grid=(b, n1),
        in_specs=[pl.BlockSpec((1, 2, n2, D), lambda bi, j: (bi, 0, 0, j)),
                  pl.BlockSpec((n2, 2 * n2), lambda bi, j: (0, 0))],
        out_specs=pl.BlockSpec((1, n2, D), lambda bi, j: (bi, 0, j)),
        out_shape=jax.ShapeDtypeStruct((b, n2, n1 * D), BF16),
        compiler_params=_cp("parallel", "parallel"),
    )(a.reshape(b, 2, n2, n1 * D), w2)
    return r.reshape(b, l, D)


def _four_direct_kernel(zr_ref, zi_ref, w_ref, o_ref):
    z = jnp.concatenate([zr_ref[0], zi_ref[0]], axis=0)
    o_ref[0] = jnp.dot(w_ref[...], z, preferred_element_type=F32).astype(BF16)


def _fourier_seq_direct(zr, zi):
    b, l, _ = zr.shape
    c, s = _dft_tables(l)
    w = jnp.asarray(np.concatenate([c, -s], axis=1) / math.sqrt(l), BF16)
    blk = pl.BlockSpec((1, l, D), lambda bi: (bi, 0, 0))
    return pl.pallas_call(
        _four_direct_kernel,
        grid=(b,),
        in_specs=[blk, blk, pl.BlockSpec((l, 2 * l), lambda bi: (0, 0))],
        out_specs=blk,
        out_shape=jax.ShapeDtypeStruct((b, l, D), BF16),
        compiler_params=_cp("parallel"),
    )(zr, zi, w)


def _merge_kernel(yf_ref, yb_ref, xs_ref, z_ref, at_ref, fo_ref, g0_ref, g1_ref, g2_ref, x_ref,
                  dsk_ref, nsw_ref, wb_ref, wo_ref, nw1_ref, gate_ref, mul2_ref, add2_ref, rwt_ref,
                  x1_ref, h2_ref, lg_ref):
    xs = xs_ref[0].astype(F32)
    z = z_ref[0].astype(F32)
    y = (yf_ref[0].astype(F32) + yb_ref[0].astype(F32) + dsk_ref[...] * xs) * (z * _sigmoid(z))
    y = (_rms(y) * nsw_ref[...]).astype(BF16)
    m = _sigmoid(g0_ref[0].astype(F32)) * jnp.dot(y, wb_ref[0], preferred_element_type=F32)
    m = m + _sigmoid(g1_ref[0].astype(F32)) * jnp.dot(at_ref[0], wb_ref[1], preferred_element_type=F32)
    m = m + _sigmoid(g2_ref[0].astype(F32)) * jnp.dot(fo_ref[0], wb_ref[2], preferred_element_type=F32)
    out = jnp.dot(m.astype(BF16), wo_ref[...], preferred_element_type=F32)
    x1 = x_ref[0] + gate_ref[0] * (_rms(out) * nw1_ref[...])
    h2 = _rms(x1) * mul2_ref[0] + add2_ref[0]
    x1_ref[0] = x1
    h2_ref[0] = h2
    lg_ref[0] = lax.dot_general(rwt_ref[...], h2, (((1,), (1,)), ((), ())), precision=HI,
                                preferred_element_type=F32)


def _merge(yf, yb, xconv, row_off, p, attn, four, x, dsk, nsw, wb, wo, nw1, gate1, mul2, add2, rwt, tm):
    b, l, _ = x.shape
    nb = l // tm
    row = lambda c=0: pl.BlockSpec((1, tm, D), lambda bi, i, c=c: (bi, i, c))
    off = lambda: pl.BlockSpec((1, tm, D), lambda bi, i: (bi, i + row_off, 0))
    vec = pl.BlockSpec((1, D), lambda bi, i: (0, 0))
    bvec = pl.BlockSpec((1, 1, D), lambda bi, i: (bi, 0, 0))
    return pl.pallas_call(
        _merge_kernel,
        grid=(b, nb),
        in_specs=[off(), off(), off(), row(COL_Z // D), row(), row(),
                  row(COL_G // D), row(COL_G // D + 1), row(COL_G // D + 2), row(),
                  vec, vec, pl.BlockSpec((3, D, D), lambda bi, i: (0, 0, 0)),
                  pl.BlockSpec((D, D), lambda bi, i: (0, 0)), vec, bvec, bvec, bvec,
                  pl.BlockSpec((N_EXPERTS, D), lambda bi, i: (0, 0))],
        out_specs=[row(), row(), pl.BlockSpec((1, N_EXPERTS, tm), lambda bi, i: (bi, 0, i))],
        out_shape=[jax.ShapeDtypeStruct((b, l, D), F32), jax.ShapeDtypeStruct((b, l, D), F32),
                   jax.ShapeDtypeStruct((b, N_EXPERTS, l), F32)],
        compiler_params=_cp("parallel", "parallel"),
    )(yf, yb, xconv, p, attn, four, p, p, p, x, dsk, nsw, wb, wo, nw1, gate1, mul2, add2, rwt)


def _router_kernel(lg_ref, bias_ref, idx_ref, gate_ref, rank_ref, cnt_ref, base_scr):
    @pl.when(pl.program_id(0) == 0)
    def _():
        base_scr[...] = jnp.zeros_like(base_scr)

    lg = lg_ref[...] + bias_ref[...]
    tr = lg.shape[1]
    eid = lax.broadcasted_iota(jnp.int32, lg.shape, 0)
    work = lg
    vals, hots = [], []
    for k in range(TOP_K):
        mx = jnp.max(work, axis=0, keepdims=True)
        sel = jnp.min(jnp.where(work == mx, eid, N_EXPERTS), axis=0, keepdims=True)
        hot = eid == sel
        idx_ref[k:k + 1, :] = sel
        vals.append(mx)
        hots.append(hot)
        work = jnp.where(hot, -jnp.inf, work)
    es = [jnp.exp(v - vals[0]) for v in vals]
    den = es[0] + es[1] + es[2] + es[3]
    for k in range(TOP_K):
        gate_ref[k:k + 1, :] = es[k] / den

    multi = jnp.zeros(lg.shape, F32)
    for hot in hots:
        multi = multi + hot.astype(F32)
    r = lax.broadcasted_iota(jnp.int32, (tr, tr), 0)
    c = lax.broadcasted_iota(jnp.int32, (tr, tr), 1)
    before = (r < c).astype(BF16)
    cnt = jnp.dot(multi.astype(BF16), before, preferred_element_type=F32) + base_scr[:, 0:1]
    for k in range(TOP_K):
        rank = jnp.sum(jnp.where(hots[k], cnt, 0.0), axis=0, keepdims=True)
        rank_ref[k:k + 1, :] = rank.astype(jnp.int32)
    base_scr[...] = base_scr[...] + jnp.sum(multi, axis=1, keepdims=True)
    cnt_ref[...] = base_scr[...]


def _route(lgt, bias, tr):
    t = lgt.shape[1]
    blk = pl.BlockSpec((TOP_K, tr), lambda i: (0, i))
    return pl.pallas_call(
        _router_kernel,
        grid=(t // tr,),
        in_specs=[pl.BlockSpec((N_EXPERTS, tr), lambda i: (0, i)), pl.BlockSpec((N_EXPERTS, 1), lambda i: (0, 0))],
        out_specs=[blk, blk, blk, pl.BlockSpec((N_EXPERTS, 128), lambda i: (0, 0))],
        out_shape=[jax.ShapeDtypeStruct((TOP_K, t), jnp.int32), jax.ShapeDtypeStruct((TOP_K, t), F32),
                   jax.ShapeDtypeStruct((TOP_K, t), jnp.int32), jax.ShapeDtypeStruct((N_EXPERTS, 128), F32)],
        scratch_shapes=[pltpu.VMEM((N_EXPERTS, 128), F32)],
        compiler_params=_cp("arbitrary"),
    )(lgt, bias)


def _row_copy(src, dst, sem):
    return pltpu.make_async_copy(src, dst, sem)


def _dispatch_kernel(dest_ref, h_ref, zero_ref, xs_ref, sem):
    del zero_ref
    td = h_ref.shape[0]

    def start(t, c):
        for k in range(TOP_K):
            _row_copy(h_ref.at[pl.ds(t, 1)], xs_ref.at[pl.ds(dest_ref[k, t], 1)], sem).start()
        return c

    lax.fori_loop(0, td, start, 0)

    def wait(t, c):
        for k in range(TOP_K):
            _row_copy(h_ref.at[pl.ds(0, 1)], xs_ref.at[pl.ds(0, 1)], sem).wait()
        return c

    lax.fori_loop(0, td, wait, 0)


def _dispatch(dest, h2, n_slots, td):
    t = h2.shape[0]
    zeros = jnp.zeros((n_slots, D), F32)
    return pl.pallas_call(
        _dispatch_kernel,
        grid=(t // td,),
        in_specs=[pl.BlockSpec((TOP_K, td), lambda i: (0, i), memory_space=pltpu.SMEM),
                  pl.BlockSpec((td, D), lambda i: (i, 0)),
                  pl.BlockSpec(memory_space=pl.ANY)],
        out_specs=pl.BlockSpec(memory_space=pl.ANY),
        out_shape=jax.ShapeDtypeStruct((n_slots, D), F32),
        scratch_shapes=[pltpu.SemaphoreType.DMA(())],
        input_output_aliases={2: 0},
        compiler_params=_cp("arbitrary"),
    )(dest, h2, zeros)


def _ffn_kernel(be_ref, nu_ref, xs_ref, w1_ref, b1_ref, w2_ref, b2_ref, ys_ref):
    del be_ref
    i = pl.program_id(0)

    @pl.when(i < nu_ref[0])
    def _():
        gu = jnp.dot(xs_ref[...].astype(BF16), w1_ref[0], preferred_element_type=F32) + b1_ref[0]
        g = jnp.minimum(gu[:, 0:D], SWIGLU_LIMIT)
        u = jnp.clip(gu[:, D:2 * D], -SWIGLU_LIMIT, SWIGLU_LIMIT)
        act = g * _sigmoid(SWIGLU_ALPHA * g) * (u + 1.0)
        ys_ref[...] = jnp.dot(act.astype(BF16), w2_ref[0], preferred_element_type=F32) + b2_ref[0]

    @pl.when(i >= nu_ref[0])
    def _():
        ys_ref[...] = jnp.zeros_like(ys_ref)


def _expert_ffn(block_e, n_used, xs, w1, b1, w2, b2):
    n_slots = xs.shape[0]
    bm = MOE_BLOCK
    grid_spec = pltpu.PrefetchScalarGridSpec(
        num_scalar_prefetch=2,
        grid=(n_slots // bm,),
        in_specs=[
            pl.BlockSpec((bm, D), lambda i, be, nu: (i, 0)),
            pl.BlockSpec((1, D, 2 * D), lambda i, be, nu: (be[i], 0, 0)),
            pl.BlockSpec((1, 1, 2 * D), lambda i, be, nu: (be[i], 0, 0)),
            pl.BlockSpec((1, D, D), lambda i, be, nu: (be[i], 0, 0)),
            pl.BlockSpec((1, 1, D), lambda i, be, nu: (be[i], 0, 0)),
        ],
        out_specs=pl.BlockSpec((bm, D), lambda i, be, nu: (i, 0)),
    )
    return pl.pallas_call(
        _ffn_kernel,
        grid_spec=grid_spec,
        out_shape=jax.ShapeDtypeStruct((n_slots, D), F32),
        compiler_params=_cp("arbitrary"),
    )(block_e, n_used, xs, w1, b1, w2, b2)


def _combine_kernel(dest_ref, gate_ref, x_ref, g2_ref, nw_ref, ys_ref, o_ref, buf, sem):
    tc = x_ref.shape[0]

    def start(t, c):
        for k in range(TOP_K):
            _row_copy(ys_ref.at[pl.ds(dest_ref[k, t], 1)], buf.at[k, pl.ds(t, 1)], sem).start()
        return c

    lax.fori_loop(0, tc, start, 0)

    def wait(t, c):
        for k in range(TOP_K):
            _row_copy(ys_ref.at[pl.ds(0, 1)], buf.at[0, pl.ds(0, 1)], sem).wait()
        return c

    lax.fori_loop(0, tc, wait, 0)
    y = gate_ref[:, 0:1] * buf[0]
    for k in range(1, TOP_K):
        y = y + gate_ref[:, k:k + 1] * buf[k]
    o_ref[...] = x_ref[...] + g2_ref[0] * (_rms(y) * nw_ref[...])


def _combine(dest, gate_t, x1, g2_blocks, nw3, ys, tok_off, tc):
    t = x1.shape[0]
    ob = tok_off // tc
    return pl.pallas_call(
        _combine_kernel,
        grid=(t // tc,),
        in_specs=[pl.BlockSpec((TOP_K, tc), lambda i: (0, i + ob), memory_space=pltpu.SMEM),
                  pl.BlockSpec((tc, TOP_K), lambda i: (i + ob, 0)),
                  pl.BlockSpec((tc, D), lambda i: (i, 0)),
                  pl.BlockSpec((1, 1, D), lambda i: (i, 0, 0)),
                  pl.BlockSpec((1, D), lambda i: (0, 0)),
                  pl.BlockSpec(memory_space=pl.ANY)],
        out_specs=pl.BlockSpec((tc, D), lambda i: (i, 0)),
        out_shape=jax.ShapeDtypeStruct((t, D), F32),
        scratch_shapes=[pltpu.VMEM((TOP_K, tc, D), F32), pltpu.SemaphoreType.DMA(())],
        compiler_params=_cp("arbitrary"),
    )(dest, gate_t, x1, g2_blocks, nw3, ys)


def _rope_tables(seqlen):
    t = jnp.arange(seqlen)
    row = (t // GRID_W).astype(F32)
    col = (t % GRID_W).astype(F32)
    inv_freq = ROPE_BASE ** (-jnp.arange(ROPE_PAIRS, dtype=F32) / ROPE_PAIRS)
    ang_r = row[:, None] * inv_freq
    ang_c = col[:, None] * inv_freq
    ang = jnp.concatenate([ang_r, ang_r, ang_c, ang_c], axis=-1)
    ang = jnp.concatenate([ang, ang], axis=-1)
    first = (jnp.arange(128) % 32) < ROPE_PAIRS
    return jnp.cos(ang), jnp.where(first, -jnp.sin(ang), jnp.sin(ang))


def _repack_w_in(w):
    main = jnp.concatenate([w[:, :XBC_W], w[:, XBC_W + DT_W:]], axis=1).astype(BF16)
    wdt = jnp.pad(w[:, XBC_W:XBC_W + DT_W], ((0, 0), (0, 128 - DT_W))).astype(BF16)
    return main, wdt


def _split_dt(dt):
    b, l, _ = dt.shape
    d = dt[:, :, :DT_W].reshape(b, l // CHUNK, CHUNK, 2, SSM_HEADS)
    dc = jnp.moveaxis(d, 3, 0)
    return dc, jnp.swapaxes(dc, -1, -2)


def _moe(lgt, router_b, h2, w1, b1, w2, b2):
    t = h2.shape[0]
    bm = MOE_BLOCK
    idx, gate, rank, cnt = _route(lgt, router_b.reshape(N_EXPERTS, 1), 256)
    counts = cnt[:, 0].astype(jnp.int32)
    padded = (counts + bm - 1) // bm * bm
    end_padded = jnp.cumsum(padded)
    start_padded = end_padded - padded
    dest = start_padded[idx] + rank
    n_slots = (t * TOP_K + bm - 1) // bm * bm + N_EXPERTS * bm
    n_blocks = n_slots // bm
    n_used = (end_padded[-1] // bm).astype(jnp.int32)
    blk = jnp.minimum(jnp.arange(n_blocks, dtype=jnp.int32), n_used - 1)
    block_e = jnp.minimum(jnp.searchsorted(end_padded, blk * bm, side='right'), N_EXPERTS - 1).astype(jnp.int32)
    xs = _dispatch(dest, h2, n_slots, 128)
    ys = _expert_ffn(block_e, n_used.reshape(1), xs, w1, b1.reshape(N_EXPERTS, 1, 2 * D), w2,
                     b2.reshape(N_EXPERTS, 1, D))
    return dest, gate.T, ys


def kernel(x, c, ctx, c_ctx, ada_w, ada_b, norm_w, w_in, conv_w, conv_b, dt_bias, a_log, d_skip, ssm_norm_w,
           lambda_qk, subln_w, w_branch, w_out, router_w, router_b, expert_w1, expert_b1, expert_w2, expert_b2):
    bsz, seqlen, _ = x.shape
    ctx_len = ctx.shape[1]
    depth = ada_w.shape[0]
    n_tok = bsz * seqlen
    rope_cos, rope_sin = _rope_tables(seqlen)
    expand = jnp.repeat(jnp.eye(SSM_HEADS, dtype=F32), SSM_P, axis=1)
    tm = min(1024, seqlen)
    tq = min(256, seqlen)

    for layer in range(depth):
        last = layer == depth - 1
        lambda_init = 0.8 - 0.6 * math.exp(-0.3 * layer)
        mod = jax.nn.silu(c) @ ada_w[layer] + ada_b[layer]
        mod_c = jax.nn.silu(c_ctx) @ ada_w[layer] + ada_b[layer]
        sh1, sc1, g1, sh2, sc2, g2 = [m[:, None, :] for m in jnp.split(mod, 6, axis=-1)]
        csh1, csc1, cg1, csh2, csc2, cg2 = [jnp.broadcast_to(m[None, None, :], (bsz, 1, D))
                                            for m in jnp.split(mod_c, 6)]
        nw = norm_w[layer]
        w_main, w_dt = _repack_w_in(w_in[layer])

        p, dt = _project(x, nw[0] * (1 + sc1), sh1, w_main, w_dt, tm)
        pc, dt_c = _project(ctx, nw[0] * (1 + csc1), csh1, w_main, w_dt, ctx_len)

        conv_wt = conv_w[layer].T
        conv_bb = conv_b[layer][None, :]
        xconv = jnp.concatenate([_conv_silu(p, conv_wt, conv_bb, min(512, seqlen)),
                                 _conv_silu(pc, conv_wt, conv_bb, ctx_len)], axis=1)
        dtc, dtr = _split_dt(jnp.concatenate([dt, dt_c], axis=1))
        a = -jnp.exp(a_log[layer].astype(F32))
        ys_dir = []
        for d in range(2):
            prow = jnp.stack([dt_bias[layer][d], a[d]], axis=0)
            ys_dir.append(_ssd_scan(xconv, dtc[d], dtr[d], prow, prow.T, expand, seqlen, reverse=(d == 1)))
        y_f, y_b = ys_dir

        lq = lambda_qk[layer].astype(F32)
        lam = (jnp.exp(jnp.sum(lq[0] * lq[1])) - jnp.exp(jnp.sum(lq[2] * lq[3])) + lambda_init).reshape(1)
        wn = (subln_w[layer] * (1.0 - lambda_init))[None, :]
        q_rot, k_rot = _rope(p, rope_cos, rope_sin, min(512, seqlen))
        kc, vc = COL_K // D, COL_V // D
        y_attn = _attention(lam, q_rot, 0, [(k_rot, 0, p, vc), (pc, kc, pc, vc)], wn, tq, 512)

        zr, zi = _fourier_channel(p, min(512, seqlen))
        y_four = _fourier_seq(zr, zi)

        dsk = jnp.repeat(d_skip[layer], SSM_P)[None, :]
        nsw = ssm_norm_w[layer][None, :]
        wb = w_branch[layer].astype(BF16)
        wo = w_out[layer].astype(BF16)
        rwt = router_w[layer].T
        tmm = min(256, seqlen)
        x1, h2, lgt = _merge(y_f, y_b, xconv, 0, p, y_attn, y_four, x, dsk, nsw, wb, wo, nw[1][None, :],
                             g1, nw[2] * (1 + sc2), sh2, rwt, tmm)
        h2 = h2.reshape(n_tok, D)
        lgt = jnp.moveaxis(lgt, 0, 1).reshape(N_EXPERTS, n_tok)

        w1 = expert_w1[layer].astype(BF16)
        w2 = expert_w2[layer].astype(BF16)
        if not last:
            y_attn_c = _attention(lam, pc, COL_Q // D, [(pc, kc, pc, vc)], wn, ctx_len, ctx_len)
            zr_c, zi_c = _fourier_channel(pc, ctx_len)
            y_four_c = _fourier_seq_direct(zr_c, zi_c)
            ctx1, hc2, lgt_c = _merge(y_f, y_b, xconv, seqlen // ctx_len, pc, y_attn_c, y_four_c, ctx, dsk, nsw,
                                      wb, wo, nw[1][None, :], cg1, nw[2] * (1 + csc2), csh2, rwt, ctx_len)
            h2 = jnp.concatenate([h2, hc2.reshape(bsz * ctx_len, D)], axis=0)
            lgt = jnp.concatenate([lgt, jnp.moveaxis(lgt_c, 0, 1).reshape(N_EXPERTS, bsz * ctx_len)], axis=1)

        dest, gate_t, ys = _moe(lgt, router_b[layer], h2, w1, expert_b1[layer], w2, expert_b2[layer])
        tc = 128
        g2_blocks = jnp.repeat(g2, seqlen // tc, axis=0)
        x = _combine(dest, gate_t, x1.reshape(n_tok, D), g2_blocks, nw[3][None, :], ys, 0, tc).reshape(x.shape)
        if not last:
            cg2_blocks = jnp.repeat(cg2, ctx_len // tc, axis=0)
            ctx = _combine(dest, gate_t, ctx1.reshape(bsz * ctx_len, D), cg2_blocks, nw[3][None, :], ys,
                           n_tok, tc).reshape(ctx.shape)
    return x
```

```python
import functools
import math

import numpy as np
import jax
import jax.numpy as jnp
from jax import lax
from jax.experimental import pallas as pl
from jax.experimental.pallas import tpu as pltpu

F32 = jnp.float32
BF16 = jnp.bfloat16
HI = lax.Precision.HIGHEST

D = 1024
GRID_W = 64
EPS = 1e-6
SSM_HEADS = 16
SSM_P = 64
N_GROUPS = 4
D_STATE = 128
CHUNK = 128
XBC_W = D + 2 * N_GROUPS * D_STATE
DT_W = 2 * SSM_HEADS
HEAD_DIM = 64
ATTN_HEADS = 8
ATTN_SCALE = HEAD_DIM ** -0.5
ROPE_BASE = 10000.0
ROPE_PAIRS = 16
FOURIER_GROUPS = 4
FOURIER_GW = 256
N_EXPERTS = 32
TOP_K = 4
SWIGLU_LIMIT = 7.0
SWIGLU_ALPHA = 1.702
LOG2E = 1.4426950408889634

COL_XBC, COL_K, COL_V, COL_Z, COL_Q, COL_F, COL_G = 0, 2048, 3072, 4096, 5120, 6144, 7168
NP = 10240

VMEM_LIMIT = 52 * 1024 * 1024
MOE_BLOCK = 256


def _cp(*sem):
    return pltpu.CompilerParams(dimension_semantics=sem, vmem_limit_bytes=VMEM_LIMIT)


def _sigmoid(x):
    return 1.0 / (1.0 + jnp.exp(-x))


def _rms(x):
    return x * lax.rsqrt(jnp.mean(x * x, axis=-1, keepdims=True) + EPS)


def _pack_pairs(x):
    half = x.shape[1] // 2
    lo = lax.bitcast_convert_type(x[:, :half].astype(BF16).astype(F32), jnp.uint32)
    hi = lax.bitcast_convert_type(x[:, half:].astype(BF16).astype(F32), jnp.uint32)
    return (lo >> 16) | (hi & jnp.uint32(0xFFFF0000))


def _unpack_pairs(u):
    lo = lax.bitcast_convert_type(u << 16, F32)
    hi = lax.bitcast_convert_type(u & jnp.uint32(0xFFFF0000), F32)
    return jnp.concatenate([lo, hi], axis=1)


def _proj_kernel(x_ref, mul_ref, add_ref, w_ref, wdt_ref, o_ref, dt_ref, h_scr):
    @pl.when(pl.program_id(2) == 0)
    def _():
        h = _rms(x_ref[0]) * mul_ref[0] + add_ref[0]
        hb = h.astype(BF16)
        h_scr[...] = hb
        dt_ref[0] = jnp.dot(hb, wdt_ref[...], preferred_element_type=F32)

    o_ref[0] = jnp.dot(h_scr[...], w_ref[...], preferred_element_type=F32).astype(BF16)


def _project(x, mul, add, w, wdt, tm):
    b, l, _ = x.shape
    tn = 1024
    return pl.pallas_call(
        _proj_kernel,
        grid=(b, l // tm, NP // tn),
        in_specs=[
            pl.BlockSpec((1, tm, D), lambda bi, i, j: (bi, i, 0)),
            pl.BlockSpec((1, 1, D), lambda bi, i, j: (bi, 0, 0)),
            pl.BlockSpec((1, 1, D), lambda bi, i, j: (bi, 0, 0)),
            pl.BlockSpec((D, tn), lambda bi, i, j: (0, j)),
            pl.BlockSpec((D, 128), lambda bi, i, j: (0, 0)),
        ],
        out_specs=[
            pl.BlockSpec((1, tm, tn), lambda bi, i, j: (bi, i, j)),
            pl.BlockSpec((1, tm, 128), lambda bi, i, j: (bi, i, 0)),
        ],
        out_shape=[jax.ShapeDtypeStruct((b, l, NP), BF16), jax.ShapeDtypeStruct((b, l, 128), F32)],
        scratch_shapes=[pltpu.VMEM((tm, D), BF16)],
        compiler_params=_cp("parallel", "parallel", "arbitrary"),
    )(x, mul, add, w, wdt)


def _conv_kernel(x_ref, prev_ref, next_ref, xc_ref, w_ref, b_ref, o_ref, ext_scr):
    i = pl.program_id(1)
    nl = pl.num_programs(1) - 1
    tr = x_ref.shape[1]
    is_ctx = i == nl
    prev = prev_ref[0].astype(F32)[8:16]
    nxt = next_ref[0].astype(F32)[0:8]
    ext_scr[0:8, :] = jnp.where(jnp.logical_and(i > 0, i < nl), prev, 0.0)
    ext_scr[8:8 + tr, :] = jnp.where(is_ctx, xc_ref[0], x_ref[0]).astype(F32)
    ext_scr[8 + tr:16 + tr, :] = jnp.where(i < nl - 1, nxt, 0.0)
    acc = b_ref[...] + w_ref[0:1, :] * ext_scr[6:6 + tr, :]
    for k in range(1, 5):
        acc = acc + w_ref[k:k + 1, :] * ext_scr[6 + k:6 + k + tr, :]
    o_ref[0] = (acc * _sigmoid(acc)).astype(BF16)


def _conv_silu(p, pc, conv_wt, conv_b):
    b, l, _ = p.shape
    tr = pc.shape[1]
    assert l % tr == 0 and tr % 16 == 0
    nl = l // tr
    nh = tr // 16
    last = l // 16 - 1
    return pl.pallas_call(
        _conv_kernel,
        grid=(b, nl + 1),
        in_specs=[
            pl.BlockSpec((1, tr, XBC_W), lambda bi, i: (bi, jnp.minimum(i, nl - 1), 0)),
            pl.BlockSpec((1, 16, XBC_W), lambda bi, i: (bi, jnp.clip(i * nh - 1, 0, last), 0)),
            pl.BlockSpec((1, 16, XBC_W), lambda bi, i: (bi, jnp.minimum((i + 1) * nh, last), 0)),
            pl.BlockSpec((1, tr, XBC_W), lambda bi, i: (bi, 0, 0)),
            pl.BlockSpec((5, XBC_W), lambda bi, i: (0, 0)),
            pl.BlockSpec((1, XBC_W), lambda bi, i: (0, 0)),
        ],
        out_specs=pl.BlockSpec((1, tr, XBC_W), lambda bi, i: (bi, i, 0)),
        out_shape=jax.ShapeDtypeStruct((b, l + tr, XBC_W), BF16),
        scratch_shapes=[pltpu.VMEM((tr + 16, XBC_W), F32)],
        compiler_params=_cp("parallel", "parallel"),
    )(p, p, p, pc, conv_wt, conv_b)


def _softplus(x):
    return jnp.maximum(x, 0.0) + jnp.log1p(jnp.exp(-jnp.abs(x)))


def _ssd_kernel(xbc_ref, dtc_ref, dtr_ref, prow_ref, pcol_ref, e_ref, o_ref, state_scr, *, reverse):
    @pl.when(pl.program_id(1) == 0)
    def _():
        state_scr[...] = jnp.zeros_like(state_scr)

    q = CHUNK
    row = lax.broadcasted_iota(jnp.int32, (q, q), 0)
    col = lax.broadcasted_iota(jnp.int32, (q, q), 1)
    keep = (col >= row) if reverse else (col <= row)
    tri = keep.astype(F32)
    tri_t = ((row >= col) if reverse else (row <= col)).astype(F32)

    dt_col = _softplus(dtc_ref[0, 0] + prow_ref[0:1, :])
    dt_row = _softplus(dtr_ref[0, 0] + pcol_ref[:, 0:1])
    adt_col = dt_col * prow_ref[1:2, :]
    adt_row = dt_row * pcol_ref[:, 1:2]
    acs_col = jnp.dot(tri, adt_col, precision=HI, preferred_element_type=F32)
    acs_row = jnp.dot(adt_row, tri_t, precision=HI, preferred_element_type=F32)
    a_tot = acs_col[0:1, :] if reverse else acs_col[q - 1:q, :]

    expand = e_ref[...]
    wend = jnp.dot(jnp.exp(a_tot - acs_col) * dt_col, expand, precision=HI, preferred_element_type=F32)
    dec_in = jnp.dot(jnp.exp(acs_col), expand, precision=HI, preferred_element_type=F32)
    dec_state = dec_in[0:1, :] if reverse else dec_in[q - 1:q, :]

    xb = xbc_ref[0, :, 0:D]
    xw = (xb.astype(F32) * wend).astype(BF16)
    lane = lax.broadcasted_iota(jnp.int32, (q, 2 * SSM_P), 1)
    gw = (SSM_HEADS // N_GROUPS) * SSM_P
    for g in range(N_GROUPS):
        bg = xbc_ref[0, :, D + g * D_STATE:D + (g + 1) * D_STATE]
        cg = xbc_ref[0, :, D + N_GROUPS * D_STATE + g * D_STATE:D + N_GROUPS * D_STATE + (g + 1) * D_STATE]
        st = state_scr[:, g * gw:(g + 1) * gw]
        y_off = jnp.dot(cg, st.astype(BF16), preferred_element_type=F32)
        s_new = lax.dot_general(bg, xw[:, g * gw:(g + 1) * gw], (((0,), (0,)), ((), ())),
                                preferred_element_type=F32)
        state_scr[:, g * gw:(g + 1) * gw] = st * dec_state[:, g * gw:(g + 1) * gw] + s_new
        cb = lax.dot_general(cg, bg, (((1,), (1,)), ((), ())), preferred_element_type=F32)
        for pair in range(2):
            h0 = 4 * g + 2 * pair
            ws = []
            for h in (h0, h0 + 1):
                seg = acs_col[:, h:h + 1] - acs_row[h:h + 1, :]
                ws.append(jnp.exp(jnp.where(keep, seg, -jnp.inf)) * cb * dt_row[h:h + 1, :])
            lhs = jnp.concatenate(ws, axis=1).astype(BF16)
            xp = xb[:, h0 * SSM_P:(h0 + 2) * SSM_P]
            zero = jnp.zeros_like(xp)
            rhs = jnp.concatenate([jnp.where(lane < SSM_P, xp, zero), jnp.where(lane >= SSM_P, xp, zero)], axis=0)
            y_diag = jnp.dot(lhs, rhs, preferred_element_type=F32)
            c0 = h0 * SSM_P
            y = y_off[:, pair * 128:(pair + 1) * 128] * dec_in[:, c0:c0 + 128] + y_diag
            o_ref[0, :, c0:c0 + 128] = y.astype(BF16)


def _ssd_scan(xbc, dtc, dtr, prow, pcol, expand, n_lat, reverse):
    b, lt, _ = xbc.shape
    nt = lt // CHUNK
    nl = n_lat // CHUNK
    nc = nt - nl
    if reverse:
        chunk = lambda s: nt - 1 - s
    else:
        chunk = lambda s: jnp.where(s < nc, nl + s, s - nc)
    return pl.pallas_call(
        functools.partial(_ssd_kernel, reverse=reverse),
        grid=(b, nt),
        in_specs=[
            pl.BlockSpec((1, CHUNK, XBC_W), lambda bi, s: (bi, chunk(s), 0)),
            pl.BlockSpec((1, 1, CHUNK, SSM_HEADS), lambda bi, s: (bi, chunk(s), 0, 0)),
            pl.BlockSpec((1, 1, SSM_HEADS, CHUNK), lambda bi, s: (bi, chunk(s), 0, 0)),
            pl.BlockSpec((2, SSM_HEADS), lambda bi, s: (0, 0)),
            pl.BlockSpec((SSM_HEADS, 2), lambda bi, s: (0, 0)),
            pl.BlockSpec((SSM_HEADS, D), lambda bi, s: (0, 0)),
        ],
        out_specs=pl.BlockSpec((1, CHUNK, D), lambda bi, s: (bi, chunk(s), 0)),
        out_shape=jax.ShapeDtypeStruct((b, lt, D), BF16),
        scratch_shapes=[pltpu.VMEM((D_STATE, D), F32)],
        compiler_params=_cp("parallel", "arbitrary"),
    )(xbc, dtc, dtr, prow, pcol, expand)


def _rope_kernel(q_ref, k_ref, v_ref, cos_ref, sin_ref, qo_ref, ko_ref, vt_ref):
    vt_ref[0] = v_ref[0].astype(F32).T.astype(BF16)
    cos = jnp.tile(cos_ref[...], (1, D // 128))
    sin = jnp.tile(sin_ref[...], (1, D // 128))
    lane = lax.broadcasted_iota(jnp.int32, cos.shape, 1)
    first = (lane % 32) < ROPE_PAIRS

    def rot(t):
        nxt = pltpu.roll(t, D - ROPE_PAIRS, axis=1)
        prv = pltpu.roll(t, ROPE_PAIRS, axis=1)
        return t * cos + jnp.where(first, nxt, prv) * sin

    qo_ref[0] = rot(q_ref[0].astype(F32)).astype(BF16)
    ko_ref[0] = rot(k_ref[0].astype(F32)).astype(BF16)


def _rope(p, cos, sin_signed, tr):
    b, l, _ = p.shape
    spec = lambda c: pl.BlockSpec((1, tr, D), lambda bi, i: (bi, i, c))
    tab = pl.BlockSpec((tr, 128), lambda bi, i: (i, 0))
    out = pl.BlockSpec((1, tr, D), lambda bi, i: (bi, i, 0))
    return pl.pallas_call(
        _rope_kernel,
        grid=(b, l // tr),
        in_specs=[spec(COL_Q // D), spec(COL_K // D), spec(COL_V // D), tab, tab],
        out_specs=[out, out, pl.BlockSpec((1, D, tr), lambda bi, i: (bi, 0, i))],
        out_shape=[jax.ShapeDtypeStruct((b, l, D), BF16)] * 2 + [jax.ShapeDtypeStruct((b, D, l), BF16)],
        compiler_params=_cp("parallel", "parallel"),
    )(p, p, p, cos, sin_signed)


ONES_ROWS = 16


def _attn_lat_kernel(lam_ref, q_ref, k_ref, vt_ref, kc_ref, vtc_ref, wn_ref, o_ref,
                     qst_scr, m_scr, acc_scr, s_scr, *, tq, tk, scale):
    q = q_ref[0].astype(F32) * scale
    lane = lax.broadcasted_iota(jnp.int32, q.shape, 1)
    qs = jnp.concatenate([jnp.where(lane < HEAD_DIM, q, 0.0), jnp.where(lane >= HEAD_DIM, q, 0.0)], axis=0)
    qst_scr[...] = qs.T.astype(BF16)
    m_scr[...] = jnp.full_like(m_scr, -1e30)
    acc_scr[...] = jnp.zeros_like(acc_scr)
    n = k_ref.shape[1] // tk

    def qk(j, slot):
        off = pl.multiple_of(j * tk, tk)
        s_scr[slot] = jnp.dot(k_ref[0, pl.ds(off, tk), :], qst_scr[...], preferred_element_type=F32)

    def soft_pv(st, vt_blk):
        vt_ext = jnp.concatenate([vt_blk, jnp.ones((ONES_ROWS, vt_blk.shape[1]), BF16)], axis=0)
        m_prev = m_scr[...]
        m_new = jnp.maximum(m_prev, jnp.max(st, axis=0, keepdims=True))
        alpha = jnp.exp2(m_prev - m_new)
        pt = jnp.exp2((st - m_new).astype(BF16))
        acc_scr[...] = alpha * acc_scr[...] + jnp.dot(vt_ext, pt, preferred_element_type=F32)
        m_scr[...] = m_new

    def use(j, slot):
        off = pl.multiple_of(j * tk, tk)
        soft_pv(s_scr[slot], vt_ref[0, :, pl.ds(off, tk)])

    qk(0, 0)

    def body(i, carry):
        j = 2 * i
        qk(j + 1, 1)
        use(j, 0)
        qk(jnp.minimum(j + 2, n - 1), 0)
        use(j + 1, 1)
        return carry

    lax.fori_loop(0, n // 2, body, 0)
    soft_pv(jnp.dot(kc_ref[0], qst_scr[...], preferred_element_type=F32), vtc_ref[0])

    acc = acc_scr[...]
    o1 = acc[0:128, 0:tq] / acc[128:129, 0:tq]
    o2 = acc[0:128, tq:2 * tq] / acc[128:129, tq:2 * tq]
    o = (o1 - lam_ref[0] * o2).T
    o_ref[0] = (_rms(o) * wn_ref[...]).astype(BF16)


def _attention_latent(lam, q, k, vt, kc, kc_col, vtc, wn, tq, tk):
    b, sq = q.shape[0], q.shape[1]
    assert (k.shape[1] // tk) % 2 == 0
    return pl.pallas_call(
        functools.partial(_attn_lat_kernel, tq=tq, tk=tk, scale=ATTN_SCALE * LOG2E),
        grid=(b, ATTN_HEADS, sq // tq),
        in_specs=[pl.BlockSpec(memory_space=pltpu.SMEM),
                  pl.BlockSpec((1, tq, 128), lambda bi, h, i: (bi, i, h)),
                  pl.BlockSpec((1, k.shape[1], 128), lambda bi, h, i: (bi, 0, h)),
                  pl.BlockSpec((1, 128, vt.shape[2]), lambda bi, h, i: (bi, h, 0)),
                  pl.BlockSpec((1, kc.shape[1], 128), lambda bi, h, i: (bi, 0, kc_col * ATTN_HEADS + h)),
                  pl.BlockSpec((1, 128, vtc.shape[2]), lambda bi, h, i: (bi, h, 0)),
                  pl.BlockSpec((1, 128), lambda bi, h, i: (0, 0))],
        out_specs=pl.BlockSpec((1, tq, 128), lambda bi, h, i: (bi, i, h)),
        out_shape=jax.ShapeDtypeStruct((b, sq, D), BF16),
        scratch_shapes=[pltpu.VMEM((128, 2 * tq), BF16), pltpu.VMEM((1, 2 * tq), F32),
                        pltpu.VMEM((128 + ONES_ROWS, 2 * tq), F32), pltpu.VMEM((2, tk, 2 * tq), F32)],
        compiler_params=_cp("parallel", "parallel", "parallel"),
    )(lam, q, k, vt, kc, vtc, wn)


def _attn_kernel(lam_ref, q_ref, *refs, n_src, tq, tk, scale):
    kv = refs[:2 * n_src]
    wn_ref, o_ref, qs_scr, m_scr, acc_scr = refs[2 * n_src:]
    q = q_ref[0].astype(F32) * scale
    lane = lax.broadcasted_iota(jnp.int32, q.shape, 1)
    qs_scr[0:tq, :] = jnp.where(lane < HEAD_DIM, q, 0.0).astype(BF16)
    qs_scr[tq:2 * tq, :] = jnp.where(lane >= HEAD_DIM, q, 0.0).astype(BF16)
    m_scr[...] = jnp.full_like(m_scr, -1e30)
    acc_scr[...] = jnp.zeros_like(acc_scr)

    def chunk(k_blk, v_blk):
        qs = qs_scr[...]
        s = lax.dot_general(qs, k_blk, (((1,), (1,)), ((), ())), preferred_element_type=F32)
        m_prev = m_scr[...]
        m_new = jnp.maximum(m_prev, jnp.max(s, axis=-1, keepdims=True))
        alpha = jnp.exp2(m_prev - m_new)
        p = jnp.exp2((s - m_new).astype(BF16))
        v_ext = jnp.concatenate([v_blk, jnp.ones_like(v_blk)], axis=1)
        acc_scr[...] = alpha * acc_scr[...] + jnp.dot(p, v_ext, preferred_element_type=F32)
        m_scr[...] = m_new

    for si in range(n_src):
        k_ref, v_ref = kv[2 * si], kv[2 * si + 1]
        n_keys = k_ref.shape[1]
        step = min(tk, n_keys)

        def body(j, carry, k_ref=k_ref, v_ref=v_ref, step=step):
            off = pl.multiple_of(j * step, step)
            chunk(k_ref[0, pl.ds(off, step), :], v_ref[0, pl.ds(off, step), :])
            return carry

        lax.fori_loop(0, n_keys // step, body, 0)

    acc = acc_scr[...]
    o1 = acc[0:tq, 0:128] / acc[0:tq, 128:256]
    o2 = acc[tq:2 * tq, 0:128] / acc[tq:2 * tq, 128:256]
    o = o1 - lam_ref[0] * o2
    o_ref[0] = (_rms(o) * wn_ref[...]).astype(BF16)


def _attention(lam, q, q_col, kvs, wn, tq, tk):
    b, sq = q.shape[0], q.shape[1]
    in_specs = [pl.BlockSpec(memory_space=pltpu.SMEM),
                pl.BlockSpec((1, tq, 128), lambda bi, h, i, c=q_col: (bi, i, c * ATTN_HEADS + h))]
    args = [lam, q]
    for k, kc, v, vc in kvs:
        in_specs.append(pl.BlockSpec((1, k.shape[1], 128), lambda bi, h, i, c=kc: (bi, 0, c * ATTN_HEADS + h)))
        in_specs.append(pl.BlockSpec((1, v.shape[1], 128), lambda bi, h, i, c=vc: (bi, 0, c * ATTN_HEADS + h)))
        args += [k, v]
    in_specs.append(pl.BlockSpec((1, 128), lambda bi, h, i: (0, 0)))
    args.append(wn)
    return pl.pallas_call(
        functools.partial(_attn_kernel, n_src=len(kvs), tq=tq, tk=tk, scale=ATTN_SCALE * LOG2E),
        grid=(b, ATTN_HEADS, sq // tq),
        in_specs=in_specs,
        out_specs=pl.BlockSpec((1, tq, 128), lambda bi, h, i: (bi, i, h)),
        out_shape=jax.ShapeDtypeStruct((b, sq, D), BF16),
        scratch_shapes=[pltpu.VMEM((2 * tq, 128), BF16), pltpu.VMEM((2 * tq, 1), F32),
                        pltpu.VMEM((2 * tq, 256), F32)],
        compiler_params=_cp("parallel", "parallel", "parallel"),
    )(*args)


def _dft_tables(n):
    k = np.arange(n)
    ang = 2.0 * np.pi * ((k[:, None] * k[None, :]) % n) / n
    return np.cos(ang), np.sin(ang)


def _four0_kernel(x_ref, w_ref, zr_ref, zi_ref):
    y = jnp.dot(x_ref[0], w_ref[...], preferred_element_type=F32)
    zr_ref[0] = y[:, 0:FOURIER_GW].astype(BF16)
    zi_ref[0] = y[:, FOURIER_GW:2 * FOURIER_GW].astype(BF16)


def _fourier_channel(p, tm):
    b, l, _ = p.shape
    c, s = _dft_tables(FOURIER_GW)
    w0 = jnp.asarray(np.concatenate([c, s], axis=1) / math.sqrt(FOURIER_GW), BF16)
    cb = COL_F // FOURIER_GW
    out = pl.BlockSpec((1, tm, FOURIER_GW), lambda bi, i, g: (bi, i, g))
    return pl.pallas_call(
        _four0_kernel,
        grid=(b, l // tm, FOURIER_GROUPS),
        in_specs=[pl.BlockSpec((1, tm, FOURIER_GW), lambda bi, i, g: (bi, i, cb + g)),
                  pl.BlockSpec((FOURIER_GW, 2 * FOURIER_GW), lambda bi, i, g: (0, 0))],
        out_specs=[out, out],
        out_shape=[jax.ShapeDtypeStruct((b, l, D), BF16)] * 2,
        compiler_params=_cp("parallel", "parallel", "parallel"),
    )(p, w0)


def _four1_kernel(zr_ref, zi_ref, m_ref, a_ref):
    n1 = zr_ref.shape[1]
    z = jnp.concatenate([zr_ref[0], zi_ref[0]], axis=0)
    a = jnp.dot(m_ref[0], z, preferred_element_type=F32).astype(BF16)
    a_ref[0, 0, 0] = a[0:n1]
    a_ref[0, 1, 0] = a[n1:2 * n1]


def _four2_kernel(a_ref, w_ref, o_ref):
    n2 = a_ref.shape[2]
    a = a_ref[0].reshape(2 * n2, a_ref.shape[-1])
    o_ref[0] = jnp.dot(w_ref[...], a, preferred_element_type=F32).astype(BF16)


def _fourier_seq(zr, zi):
    b, l, _ = zr.shape
    n2 = GRID_W
    n1 = l // n2
    k1 = np.arange(n1)
    ang1 = 2.0 * np.pi * ((k1[:, None] * k1[None, :]) % n1) / n1
    tw = 2.0 * np.pi * ((np.arange(n2)[:, None] * k1[None, :]) % l) / l
    ang = ang1[None, :, :] + tw[:, :, None]
    cr, si = np.cos(ang) / math.sqrt(n1), np.sin(ang) / math.sqrt(n1)
    m = jnp.asarray(np.concatenate([np.concatenate([cr, -si], axis=2),
                                    np.concatenate([si, cr], axis=2)], axis=1), BF16)
    c2, s2 = _dft_tables(n2)
    w2 = jnp.asarray(np.concatenate([c2, -s2], axis=1) / math.sqrt(n2), BF16)

    zin = pl.BlockSpec((1, n1, D), lambda bi, j: (bi, 0, j))
    a = pl.pallas_call(
        _four1_kernel,
        grid=(b, n2),
        in_specs=[zin, zin, pl.BlockSpec((1, 2 * n1, 2 * n1), lambda bi, j: (j, 0, 0))],
        out_specs=pl.BlockSpec((1, 2, 1, n1, D), lambda bi, j: (bi, 0, j, 0, 0)),
        out_shape=jax.ShapeDtypeStruct((b, 2, n2, n1, D), BF16),
        compiler_params=_cp("parallel", "parallel"),
    )(zr.reshape(b, n1, n2 * D), zi.reshape(b, n1, n2 * D), m)
    r = pl.pallas_call(
        _four2_kernel,
        grid=(b, n1),
        in_specs=[pl.BlockSpec((1, 2, n2, D), lambda bi, j: (bi, 0, 0, j)),
                  pl.BlockSpec((n2, 2 * n2), lambda bi, j: (0, 0))],
        out_specs=pl.BlockSpec((1, n2, D), lambda bi, j: (bi, 0, j)),
        out_shape=jax.ShapeDtypeStruct((b, n2, n1 * D), BF16),
        compiler_params=_cp("parallel", "parallel"),
    )(a.reshape(b, 2, n2, n1 * D), w2)
    return r.reshape(b, l, D)


def _four_direct_kernel(zr_ref, zi_ref, w_ref, o_ref):
    z = jnp.concatenate([zr_ref[0], zi_ref[0]], axis=0)
    o_ref[0] = jnp.dot(w_ref[...], z, preferred_element_type=F32).astype(BF16)


def _fourier_seq_direct(zr, zi):
    b, l, _ = zr.shape
    c, s = _dft_tables(l)
    w = jnp.asarray(np.concatenate([c, -s], axis=1) / math.sqrt(l), BF16)
    blk = pl.BlockSpec((1, l, D), lambda bi: (bi, 0, 0))
    return pl.pallas_call(
        _four_direct_kernel,
        grid=(b,),
        in_specs=[blk, blk, pl.BlockSpec((l, 2 * l), lambda bi: (0, 0))],
        out_specs=blk,
        out_shape=jax.ShapeDtypeStruct((b, l, D), BF16),
        compiler_params=_cp("parallel"),
    )(zr, zi, w)


def _merge_kernel(yf_ref, yb_ref, xs_ref, z_ref, at_ref, fo_ref, g0_ref, g1_ref, g2_ref, x_ref,
                  dsk_ref, nsw_ref, wb_ref, wo_ref, nw1_ref, gate_ref, mul2_ref, add2_ref, rwt_ref,
                  x1_ref, h2_ref, lg_ref):
    xs = xs_ref[0].astype(F32)
    z = z_ref[0].astype(F32)
    y = (yf_ref[0].astype(F32) + yb_ref[0].astype(F32) + dsk_ref[...] * xs) * (z * _sigmoid(z))
    y = (_rms(y) * nsw_ref[...]).astype(BF16)
    m = _sigmoid(g0_ref[0].astype(F32)) * jnp.dot(y, wb_ref[0], preferred_element_type=F32)
    m = m + _sigmoid(g1_ref[0].astype(F32)) * jnp.dot(at_ref[0], wb_ref[1], preferred_element_type=F32)
    m = m + _sigmoid(g2_ref[0].astype(F32)) * jnp.dot(fo_ref[0], wb_ref[2], preferred_element_type=F32)
    out = jnp.dot(m.astype(BF16), wo_ref[...], preferred_element_type=F32)
    x1 = x_ref[0] + gate_ref[0] * (_rms(out) * nw1_ref[...])
    h2 = _rms(x1) * mul2_ref[0] + add2_ref[0]
    x1_ref[0] = x1
    h2_ref[0] = _pack_pairs(h2)
    lg_ref[0] = lax.dot_general(rwt_ref[...], h2, (((1,), (1,)), ((), ())), precision=HI,
                                preferred_element_type=F32)


def _merge(yf, yb, xconv, row_off, p, attn, four, x, dsk, nsw, wb, wo, nw1, gate1, mul2, add2, rwt, tm):
    b, l, _ = x.shape
    nb = l // tm
    row = lambda c=0: pl.BlockSpec((1, tm, D), lambda bi, i, c=c: (bi, i, c))
    off = lambda: pl.BlockSpec((1, tm, D), lambda bi, i: (bi, i + row_off, 0))
    vec = pl.BlockSpec((1, D), lambda bi, i: (0, 0))
    bvec = pl.BlockSpec((1, 1, D), lambda bi, i: (bi, 0, 0))
    return pl.pallas_call(
        _merge_kernel,
        grid=(b, nb),
        in_specs=[off(), off(), off(), row(COL_Z // D), row(), row(),
                  row(COL_G // D), row(COL_G // D + 1), row(COL_G // D + 2), row(),
                  vec, vec, pl.BlockSpec((3, D, D), lambda bi, i: (0, 0, 0)),
                  pl.BlockSpec((D, D), lambda bi, i: (0, 0)), vec, bvec, bvec, bvec,
                  pl.BlockSpec((N_EXPERTS, D), lambda bi, i: (0, 0))],
        out_specs=[row(), pl.BlockSpec((1, tm, D // 2), lambda bi, i: (bi, i, 0)),
                   pl.BlockSpec((1, N_EXPERTS, tm), lambda bi, i: (bi, 0, i))],
        out_shape=[jax.ShapeDtypeStruct((b, l, D), F32), jax.ShapeDtypeStruct((b, l, D // 2), jnp.uint32),
                   jax.ShapeDtypeStruct((b, N_EXPERTS, l), F32)],
        compiler_params=_cp("parallel", "parallel"),
    )(yf, yb, xconv, p, attn, four, p, p, p, x, dsk, nsw, wb, wo, nw1, gate1, mul2, add2, rwt)


def _router_kernel(lg_ref, bias_ref, idx_ref, gate_ref, rank_ref, cnt_ref, base_scr):
    @pl.when(pl.program_id(0) == 0)
    def _():
        base_scr[...] = jnp.zeros_like(base_scr)

    lg = lg_ref[...] + bias_ref[...]
    tr = lg.shape[1]
    eid = lax.broadcasted_iota(jnp.int32, lg.shape, 0)
    work = lg
    vals, hots = [], []
    for k in range(TOP_K):
        mx = jnp.max(work, axis=0, keepdims=True)
        sel = jnp.min(jnp.where(work == mx, eid, N_EXPERTS), axis=0, keepdims=True)
        hot = eid == sel
        idx_ref[k:k + 1, :] = sel
        vals.append(mx)
        hots.append(hot)
        work = jnp.where(hot, -jnp.inf, work)
    es = [jnp.exp(v - vals[0]) for v in vals]
    den = es[0] + es[1] + es[2] + es[3]
    for k in range(TOP_K):
        gate_ref[k:k + 1, :] = es[k] / den

    multi = jnp.zeros(lg.shape, F32)
    for hot in hots:
        multi = multi + hot.astype(F32)
    r = lax.broadcasted_iota(jnp.int32, (tr, tr), 0)
    c = lax.broadcasted_iota(jnp.int32, (tr, tr), 1)
    before = (r < c).astype(BF16)
    cnt = jnp.dot(multi.astype(BF16), before, preferred_element_type=F32) + base_scr[:, 0:1]
    for k in range(TOP_K):
        rank = jnp.sum(jnp.where(hots[k], cnt, 0.0), axis=0, keepdims=True)
        rank_ref[k:k + 1, :] = rank.astype(jnp.int32)
    base_scr[...] = base_scr[...] + jnp.sum(multi, axis=1, keepdims=True)
    cnt_ref[...] = base_scr[...]


def _route(lgt, bias, tr):
    t = lgt.shape[1]
    blk = pl.BlockSpec((TOP_K, tr), lambda i: (0, i))
    return pl.pallas_call(
        _router_kernel,
        grid=(t // tr,),
        in_specs=[pl.BlockSpec((N_EXPERTS, tr), lambda i: (0, i)), pl.BlockSpec((N_EXPERTS, 1), lambda i: (0, 0))],
        out_specs=[blk, blk, blk, pl.BlockSpec((N_EXPERTS, 128), lambda i: (0, 0))],
        out_shape=[jax.ShapeDtypeStruct((TOP_K, t), jnp.int32), jax.ShapeDtypeStruct((TOP_K, t), F32),
                   jax.ShapeDtypeStruct((TOP_K, t), jnp.int32), jax.ShapeDtypeStruct((N_EXPERTS, 128), F32)],
        scratch_shapes=[pltpu.VMEM((N_EXPERTS, 128), F32)],
        compiler_params=_cp("arbitrary"),
    )(lgt, bias)


def _row_copy(src, dst, sem):
    return pltpu.make_async_copy(src, dst, sem)


def _dispatch_kernel(dest_ref, h_ref, zero_ref, xs_ref, sem):
    del zero_ref
    td = h_ref.shape[0]

    def start(t, c):
        for k in range(TOP_K):
            _row_copy(h_ref.at[pl.ds(t, 1)], xs_ref.at[pl.ds(dest_ref[k, t], 1)], sem).start()
        return c

    lax.fori_loop(0, td, start, 0)

    def wait(t, c):
        for k in range(TOP_K):
            _row_copy(h_ref.at[pl.ds(0, 1)], xs_ref.at[pl.ds(0, 1)], sem).wait()
        return c

    lax.fori_loop(0, td, wait, 0)


def _dispatch(dest, h2p, xs_init, tok_off, td):
    t = h2p.shape[0]
    ob = tok_off // td
    return pl.pallas_call(
        _dispatch_kernel,
        grid=(t // td,),
        in_specs=[pl.BlockSpec((TOP_K, td), lambda i: (0, i + ob), memory_space=pltpu.SMEM),
                  pl.BlockSpec((td, D // 2), lambda i: (i, 0)),
                  pl.BlockSpec(memory_space=pl.ANY)],
        out_specs=pl.BlockSpec(memory_space=pl.ANY),
        out_shape=jax.ShapeDtypeStruct(xs_init.shape, xs_init.dtype),
        scratch_shapes=[pltpu.SemaphoreType.DMA(())],
        input_output_aliases={2: 0},
        compiler_params=_cp("arbitrary"),
    )(dest, h2p, xs_init)


def _ffn_kernel(be_ref, nu_ref, xs_ref, w1_ref, b1_ref, w2_ref, b2_ref, ys_ref, w1b_scr, w2b_scr):
    i = pl.program_id(0)
    new_expert = jnp.logical_or(i == 0, be_ref[i] != be_ref[jnp.maximum(i - 1, 0)])

    @pl.when(new_expert)
    def _():
        w1b_scr[...] = w1_ref[0, 0].astype(BF16)
        w2b_scr[...] = w2_ref[0, 0].astype(BF16)

    @pl.when(i < nu_ref[0])
    def _():
        x = _unpack_pairs(xs_ref[...]).astype(BF16)
        gu = jnp.dot(x, w1b_scr[...], preferred_element_type=F32) + b1_ref[0, 0]
        g = jnp.minimum(gu[:, 0:D], SWIGLU_LIMIT)
        u = jnp.clip(gu[:, D:2 * D], -SWIGLU_LIMIT, SWIGLU_LIMIT)
        act = g * _sigmoid(SWIGLU_ALPHA * g) * (u + 1.0)
        y = jnp.dot(act.astype(BF16), w2b_scr[...], preferred_element_type=F32) + b2_ref[0, 0]
        ys_ref[...] = _pack_pairs(y)

    @pl.when(i >= nu_ref[0])
    def _():
        ys_ref[...] = jnp.zeros_like(ys_ref)


def _expert_ffn(block_e, n_used, xs, w1, b1, w2, b2, layer):
    n_slots = xs.shape[0]
    bm = MOE_BLOCK
    grid_spec = pltpu.PrefetchScalarGridSpec(
        num_scalar_prefetch=2,
        grid=(n_slots // bm,),
        in_specs=[
            pl.BlockSpec((bm, D // 2), lambda i, be, nu: (i, 0)),
            pl.BlockSpec((1, 1, D, 2 * D), lambda i, be, nu: (layer, be[i], 0, 0)),
            pl.BlockSpec((1, 1, 1, 2 * D), lambda i, be, nu: (layer, be[i], 0, 0)),
            pl.BlockSpec((1, 1, D, D), lambda i, be, nu: (layer, be[i], 0, 0)),
            pl.BlockSpec((1, 1, 1, D), lambda i, be, nu: (layer, be[i], 0, 0)),
        ],
        out_specs=pl.BlockSpec((bm, D // 2), lambda i, be, nu: (i, 0)),
        scratch_shapes=[pltpu.VMEM((D, 2 * D), BF16), pltpu.VMEM((D, D), BF16)],
    )
    return pl.pallas_call(
        _ffn_kernel,
        grid_spec=grid_spec,
        out_shape=jax.ShapeDtypeStruct((n_slots, D // 2), jnp.uint32),
        compiler_params=_cp("arbitrary"),
    )(block_e, n_used, xs, w1, b1, w2, b2)


def _combine_kernel(dest_ref, gate_ref, x_ref, g2_ref, nw_ref, ys_ref, o_ref, buf, sem):
    tc = x_ref.shape[0]

    def start(t, c):
        for k in range(TOP_K):
            _row_copy(ys_ref.at[pl.ds(dest_ref[k, t], 1)], buf.at[k, pl.ds(t, 1)], sem).start()
        return c

    lax.fori_loop(0, tc, start, 0)

    def wait(t, c):
        for k in range(TOP_K):
            _row_copy(ys_ref.at[pl.ds(0, 1)], buf.at[0, pl.ds(0, 1)], sem).wait()
        return c

    lax.fori_loop(0, tc, wait, 0)
    y = gate_ref[:, 0:1] * _unpack_pairs(buf[0])
    for k in range(1, TOP_K):
        y = y + gate_ref[:, k:k + 1] * _unpack_pairs(buf[k])
    o_ref[...] = x_ref[...] + g2_ref[0] * (_rms(y) * nw_ref[...])


def _combine(dest, gate_t, x1, g2_blocks, nw3, ys, tok_off, tc):
    t = x1.shape[0]
    ob = tok_off // tc
    return pl.pallas_call(
        _combine_kernel,
        grid=(t // tc,),
        in_specs=[pl.BlockSpec((TOP_K, tc), lambda i: (0, i + ob), memory_space=pltpu.SMEM),
                  pl.BlockSpec((tc, TOP_K), lambda i: (i + ob, 0)),
                  pl.BlockSpec((tc, D), lambda i: (i, 0)),
                  pl.BlockSpec((1, 1, D), lambda i: (i, 0, 0)),
                  pl.BlockSpec((1, D), lambda i: (0, 0)),
                  pl.BlockSpec(memory_space=pl.ANY)],
        out_specs=pl.BlockSpec((tc, D), lambda i: (i, 0)),
        out_shape=jax.ShapeDtypeStruct((t, D), F32),
        scratch_shapes=[pltpu.VMEM((TOP_K, tc, D // 2), jnp.uint32), pltpu.SemaphoreType.DMA(())],
        compiler_params=_cp("arbitrary"),
    )(dest, gate_t, x1, g2_blocks, nw3, ys)


def _rope_tables(seqlen):
    t = jnp.arange(seqlen)
    row = (t // GRID_W).astype(F32)
    col = (t % GRID_W).astype(F32)
    inv_freq = ROPE_BASE ** (-jnp.arange(ROPE_PAIRS, dtype=F32) / ROPE_PAIRS)
    ang_r = row[:, None] * inv_freq
    ang_c = col[:, None] * inv_freq
    ang = jnp.concatenate([ang_r, ang_r, ang_c, ang_c], axis=-1)
    ang = jnp.concatenate([ang, ang], axis=-1)
    first = (jnp.arange(128) % 32) < ROPE_PAIRS
    return jnp.cos(ang), jnp.where(first, -jnp.sin(ang), jnp.sin(ang))


def _repack_w_in(w):
    main = jnp.concatenate([w[:, :XBC_W], w[:, XBC_W + DT_W:]], axis=1).astype(BF16)
    wdt = jnp.pad(w[:, XBC_W:XBC_W + DT_W], ((0, 0), (0, 128 - DT_W))).astype(BF16)
    return main, wdt


def _split_dt(dt):
    b, l, _ = dt.shape
    d = dt[:, :, :DT_W].reshape(b, l // CHUNK, CHUNK, 2, SSM_HEADS)
    dc = jnp.moveaxis(d, 3, 0)
    return dc, jnp.swapaxes(dc, -1, -2)


def _moe(lgt, router_b, h2_parts, w1, b1, w2, b2, layer):
    t = lgt.shape[1]
    bm = MOE_BLOCK
    idx, gate, rank, cnt = _route(lgt, router_b.reshape(N_EXPERTS, 1), 256)
    counts = cnt[:, 0].astype(jnp.int32)
    padded = (counts + bm - 1) // bm * bm
    end_padded = jnp.cumsum(padded)
    start_padded = end_padded - padded
    experts = jnp.arange(N_EXPERTS, dtype=jnp.int32)[:, None, None]
    dest = rank + jnp.sum(jnp.where(idx[None] == experts, start_padded[:, None, None], 0), axis=0)
    n_slots = (t * TOP_K + bm - 1) // bm * bm + N_EXPERTS * bm
    n_blocks = n_slots // bm
    n_used = (end_padded[-1] // bm).astype(jnp.int32)
    blk = jnp.minimum(jnp.arange(n_blocks, dtype=jnp.int32), n_used - 1)
    block_e = jnp.sum((end_padded[None, :] <= (blk * bm)[:, None]).astype(jnp.int32), axis=1)
    block_e = jnp.minimum(block_e, N_EXPERTS - 1)
    xs = jnp.zeros((n_slots, D // 2), jnp.uint32)
    off = 0
    for part in h2_parts:
        xs = _dispatch(dest, part, xs, off, 128)
        off += part.shape[0]
    ys = _expert_ffn(block_e, n_used.reshape(1), xs, w1, b1[:, :, None, :], w2, b2[:, :, None, :], layer)
    return dest, gate.T, ys


def kernel(x, c, ctx, c_ctx, ada_w, ada_b, norm_w, w_in, conv_w, conv_b, dt_bias, a_log, d_skip, ssm_norm_w,
           lambda_qk, subln_w, w_branch, w_out, router_w, router_b, expert_w1, expert_b1, expert_w2, expert_b2):
    bsz, seqlen, _ = x.shape
    ctx_len = ctx.shape[1]
    depth = ada_w.shape[0]
    n_tok = bsz * seqlen
    rope_cos, rope_sin = _rope_tables(seqlen)
    expand = jnp.repeat(jnp.eye(SSM_HEADS, dtype=F32), SSM_P, axis=1)
    tm = min(1024, seqlen)
    tq = min(512, seqlen)

    for layer in range(depth):
        last = layer == depth - 1
        lambda_init = 0.8 - 0.6 * math.exp(-0.3 * layer)
        mod = jax.nn.silu(c) @ ada_w[layer] + ada_b[layer]
        mod_c = jax.nn.silu(c_ctx) @ ada_w[layer] + ada_b[layer]
        sh1, sc1, g1, sh2, sc2, g2 = [m[:, None, :] for m in jnp.split(mod, 6, axis=-1)]
        csh1, csc1, cg1, csh2, csc2, cg2 = [jnp.broadcast_to(m[None, None, :], (bsz, 1, D))
                                            for m in jnp.split(mod_c, 6)]
        nw = norm_w[layer]
        w_main, w_dt = _repack_w_in(w_in[layer])

        p, dt = _project(x, nw[0] * (1 + sc1), sh1, w_main, w_dt, tm)
        pc, dt_c = _project(ctx, nw[0] * (1 + csc1), csh1, w_main, w_dt, ctx_len)

        conv_wt = conv_w[layer].T
        conv_bb = conv_b[layer][None, :]
        xconv = _conv_silu(p, pc, conv_wt, conv_bb)
        dtc, dtr = _split_dt(jnp.concatenate([dt, dt_c], axis=1))
        a = -jnp.exp(a_log[layer].astype(F32))
        ys_dir = []
        for d in range(2):
            prow = jnp.stack([dt_bias[layer][d], a[d]], axis=0)
            ys_dir.append(_ssd_scan(xconv, dtc[d], dtr[d], prow, prow.T, expand, seqlen, reverse=(d == 1)))
        y_f, y_b = ys_dir

        lq = lambda_qk[layer].astype(F32)
        lam = (jnp.exp(jnp.sum(lq[0] * lq[1])) - jnp.exp(jnp.sum(lq[2] * lq[3])) + lambda_init).reshape(1)
        wn = (subln_w[layer] * (1.0 - lambda_init))[None, :]
        q_rot, k_rot, v_t = _rope(p, rope_cos, rope_sin, min(512, seqlen))
        kc, vc = COL_K // D, COL_V // D
        v_t_ctx = jnp.swapaxes(pc[:, :, COL_V:COL_V + D], 1, 2)
        y_attn = _attention_latent(lam, q_rot, k_rot, v_t, pc, kc, v_t_ctx, wn, tq, min(512, seqlen // 2))

        zr, zi = _fourier_channel(p, min(512, seqlen))
        y_four = _fourier_seq(zr, zi)

        dsk = jnp.repeat(d_skip[layer], SSM_P)[None, :]
        nsw = ssm_norm_w[layer][None, :]
        wb = w_branch[layer].astype(BF16)
        wo = w_out[layer].astype(BF16)
        rwt = router_w[layer].T
        tmm = min(256, seqlen)
        x1, h2, lgt = _merge(y_f, y_b, xconv, 0, p, y_attn, y_four, x, dsk, nsw, wb, wo, nw[1][None, :],
                             g1, nw[2] * (1 + sc2), sh2, rwt, tmm)
        h2_parts = [h2.reshape(n_tok, D // 2)]
        lgt = jnp.moveaxis(lgt, 0, 1).reshape(N_EXPERTS, n_tok)

        if not last:
            y_attn_c = _attention(lam, pc, COL_Q // D, [(pc, kc, pc, vc)], wn, ctx_len, ctx_len)
            zr_c, zi_c = _fourier_channel(pc, ctx_len)
            y_four_c = _fourier_seq_direct(zr_c, zi_c)
            ctx1, hc2, lgt_c = _merge(y_f, y_b, xconv, seqlen // ctx_len, pc, y_attn_c, y_four_c, ctx, dsk, nsw,
                                      wb, wo, nw[1][None, :], cg1, nw[2] * (1 + csc2), csh2, rwt, ctx_len)
            h2_parts.append(hc2.reshape(bsz * ctx_len, D // 2))
            lgt = jnp.concatenate([lgt, jnp.moveaxis(lgt_c, 0, 1).reshape(N_EXPERTS, bsz * ctx_len)], axis=1)

        dest, gate_t, ys = _moe(lgt, router_b[layer], h2_parts, expert_w1, expert_b1, expert_w2, expert_b2, layer)
        tc = 128
        g2_blocks = jnp.repeat(g2, seqlen // tc, axis=0)
        x = _combine(dest, gate_t, x1.reshape(n_tok, D), g2_blocks, nw[3][None, :], ys, 0, tc).reshape(x.shape)
        if not last:
            cg2_blocks = jnp.repeat(cg2, ctx_len // tc, axis=0)
            ctx = _combine(dest, gate_t, ctx1.reshape(bsz * ctx_len, D), cg2_blocks, nw[3][None, :], ys,
                           n_tok, tc).reshape(ctx.shape)
    return x
```

```python
import functools
import math

import numpy as np
import jax
import jax.numpy as jnp
from jax import lax
from jax.experimental import pallas as pl
from jax.experimental.pallas import tpu as pltpu

F32 = jnp.float32
BF16 = jnp.bfloat16
HI = lax.Precision.HIGHEST

D = 1024
GRID_W = 64
EPS = 1e-6
SSM_HEADS = 16
SSM_P = 64
N_GROUPS = 4
D_STATE = 128
CHUNK = 128
XBC_W = D + 2 * N_GROUPS * D_STATE
DT_W = 2 * SSM_HEADS
HEAD_DIM = 64
ATTN_HEADS = 8
ATTN_SCALE = HEAD_DIM ** -0.5
ROPE_BASE = 10000.0
ROPE_PAIRS = 16
FOURIER_GROUPS = 4
FOURIER_GW = 256
N_EXPERTS = 32
TOP_K = 4
SWIGLU_LIMIT = 7.0
SWIGLU_ALPHA = 1.702
LOG2E = 1.4426950408889634

COL_XBC, COL_K, COL_V, COL_Z, COL_Q, COL_F, COL_G = 0, 2048, 3072, 4096, 5120, 6144, 7168
NP = 10240

VMEM_LIMIT = 52 * 1024 * 1024
MOE_BLOCK = 256


def _cp(*sem):
    return pltpu.CompilerParams(dimension_semantics=sem, vmem_limit_bytes=VMEM_LIMIT)


def _sigmoid(x):
    return 0.5 * jnp.tanh(0.5 * x) + 0.5


def _rms(x):
    return x * lax.rsqrt(jnp.mean(x * x, axis=-1, keepdims=True) + EPS)


def _pack_pairs(x):
    half = x.shape[1] // 2
    lo = lax.bitcast_convert_type(x[:, :half].astype(BF16).astype(F32), jnp.uint32)
    hi = lax.bitcast_convert_type(x[:, half:].astype(BF16).astype(F32), jnp.uint32)
    return (lo >> 16) | (hi & jnp.uint32(0xFFFF0000))


def _unpack_pairs(u):
    lo = lax.bitcast_convert_type(u << 16, F32)
    hi = lax.bitcast_convert_type(u & jnp.uint32(0xFFFF0000), F32)
    return jnp.concatenate([lo, hi], axis=1)


def _proj_kernel(x_ref, mul_ref, add_ref, w_ref, wdt_ref, o_ref, dt_ref, h_scr):
    @pl.when(pl.program_id(2) == 0)
    def _():
        h = _rms(x_ref[0]) * mul_ref[0] + add_ref[0]
        hb = h.astype(BF16)
        h_scr[...] = hb
        dt_ref[0] = jnp.dot(hb, wdt_ref[...], preferred_element_type=F32)

    o_ref[0] = jnp.dot(h_scr[...], w_ref[...], preferred_element_type=F32).astype(BF16)


def _project(x, mul, add, w, wdt, tm):
    b, l, _ = x.shape
    tn = 1024
    return pl.pallas_call(
        _proj_kernel,
        grid=(b, l // tm, NP // tn),
        in_specs=[
            pl.BlockSpec((1, tm, D), lambda bi, i, j: (bi, i, 0)),
            pl.BlockSpec((1, 1, D), lambda bi, i, j: (bi, 0, 0)),
            pl.BlockSpec((1, 1, D), lambda bi, i, j: (bi, 0, 0)),
            pl.BlockSpec((D, tn), lambda bi, i, j: (0, j)),
            pl.BlockSpec((D, 128), lambda bi, i, j: (0, 0)),
        ],
        out_specs=[
            pl.BlockSpec((1, tm, tn), lambda bi, i, j: (bi, i, j)),
            pl.BlockSpec((1, tm, 128), lambda bi, i, j: (bi, i, 0)),
        ],
        out_shape=[jax.ShapeDtypeStruct((b, l, NP), BF16), jax.ShapeDtypeStruct((b, l, 128), F32)],
        scratch_shapes=[pltpu.VMEM((tm, D), BF16)],
        compiler_params=_cp("parallel", "parallel", "arbitrary"),
    )(x, mul, add, w, wdt)


def _conv_kernel(x_ref, prev_ref, next_ref, xc_ref, w_ref, b_ref, o_ref, ext_scr):
    i = pl.program_id(1)
    nl = pl.num_programs(1) - 1
    tr = x_ref.shape[1]
    is_ctx = i == nl
    prev = prev_ref[0].astype(F32)[8:16]
    nxt = next_ref[0].astype(F32)[0:8]
    ext_scr[0:8, :] = jnp.where(jnp.logical_and(i > 0, i < nl), prev, 0.0)
    ext_scr[8:8 + tr, :] = jnp.where(is_ctx, xc_ref[0], x_ref[0]).astype(F32)
    ext_scr[8 + tr:16 + tr, :] = jnp.where(i < nl - 1, nxt, 0.0)
    acc = b_ref[...] + w_ref[0:1, :] * ext_scr[6:6 + tr, :]
    for k in range(1, 5):
        acc = acc + w_ref[k:k + 1, :] * ext_scr[6 + k:6 + k + tr, :]
    o_ref[0] = (acc * _sigmoid(acc)).astype(BF16)


def _conv_silu(p, pc, conv_wt, conv_b):
    b, l, _ = p.shape
    tr = pc.shape[1]
    assert l % tr == 0 and tr % 16 == 0
    nl = l // tr
    nh = tr // 16
    last = l // 16 - 1
    return pl.pallas_call(
        _conv_kernel,
        grid=(b, nl + 1),
        in_specs=[
            pl.BlockSpec((1, tr, XBC_W), lambda bi, i: (bi, jnp.minimum(i, nl - 1), 0)),
            pl.BlockSpec((1, 16, XBC_W), lambda bi, i: (bi, jnp.clip(i * nh - 1, 0, last), 0)),
            pl.BlockSpec((1, 16, XBC_W), lambda bi, i: (bi, jnp.minimum((i + 1) * nh, last), 0)),
            pl.BlockSpec((1, tr, XBC_W), lambda bi, i: (bi, 0, 0)),
            pl.BlockSpec((5, XBC_W), lambda bi, i: (0, 0)),
            pl.BlockSpec((1, XBC_W), lambda bi, i: (0, 0)),
        ],
        out_specs=pl.BlockSpec((1, tr, XBC_W), lambda bi, i: (bi, i, 0)),
        out_shape=jax.ShapeDtypeStruct((b, l + tr, XBC_W), BF16),
        scratch_shapes=[pltpu.VMEM((tr + 16, XBC_W), F32)],
        compiler_params=_cp("parallel", "parallel"),
    )(p, p, p, pc, conv_wt, conv_b)


def _softplus(x):
    return jnp.maximum(x, 0.0) + jnp.log1p(jnp.exp(-jnp.abs(x)))


def _split3(v):
    hi = v.astype(BF16)
    r = v - hi.astype(F32)
    mid = r.astype(BF16)
    lo = (r - mid.astype(F32)).astype(BF16)
    return hi, mid, lo


def _ssd_kernel(xbc_ref, dtc_ref, dtr_ref, prow_ref, pcol_ref, e_ref, o_ref, state_scr, *, reverse):
    @pl.when(pl.program_id(1) == 0)
    def _():
        state_scr[...] = jnp.zeros_like(state_scr)

    q = CHUNK
    row = lax.broadcasted_iota(jnp.int32, (q, q), 0)
    col = lax.broadcasted_iota(jnp.int32, (q, q), 1)
    keep = (col >= row) if reverse else (col <= row)
    tri = keep.astype(BF16)
    tri_t = ((row >= col) if reverse else (row <= col)).astype(BF16)
    nh = SSM_HEADS

    dt_col = _softplus(dtc_ref[0, 0] + prow_ref[0:1, :])
    dt_row = _softplus(dtr_ref[0, 0] + pcol_ref[:, 0:1])
    adt_col = dt_col * prow_ref[1:2, :]
    adt_row = dt_row * pcol_ref[:, 1:2]
    c3 = jnp.dot(tri, jnp.concatenate(_split3(adt_col), axis=1), preferred_element_type=F32)
    acs_col = c3[:, 0:nh] + c3[:, nh:2 * nh] + c3[:, 2 * nh:3 * nh]
    r3 = jnp.dot(jnp.concatenate(_split3(adt_row), axis=0), tri_t, preferred_element_type=F32)
    acs_row = r3[0:nh] + r3[nh:2 * nh] + r3[2 * nh:3 * nh]
    a_tot = acs_col[0:1, :] if reverse else acs_col[q - 1:q, :]

    expand3 = e_ref[...]

    def expand(v):
        return jnp.dot(jnp.concatenate(_split3(v), axis=1), expand3, preferred_element_type=F32)

    wend = expand(jnp.exp(a_tot - acs_col) * dt_col)
    dec_in = expand(jnp.exp(acs_col))
    dec_state = dec_in[0:1, :] if reverse else dec_in[q - 1:q, :]

    xb = xbc_ref[0, :, 0:D]
    xw = (xb.astype(F32) * wend).astype(BF16)
    lane = lax.broadcasted_iota(jnp.int32, (q, 2 * SSM_P), 1)
    gw = (SSM_HEADS // N_GROUPS) * SSM_P
    for g in range(N_GROUPS):
        bg = xbc_ref[0, :, D + g * D_STATE:D + (g + 1) * D_STATE]
        cg = xbc_ref[0, :, D + N_GROUPS * D_STATE + g * D_STATE:D + N_GROUPS * D_STATE + (g + 1) * D_STATE]
        st = state_scr[:, g * gw:(g + 1) * gw]
        y_off = jnp.dot(cg, st.astype(BF16), preferred_element_type=F32)
        s_new = lax.dot_general(bg, xw[:, g * gw:(g + 1) * gw], (((0,), (0,)), ((), ())),
                                preferred_element_type=F32)
        state_scr[:, g * gw:(g + 1) * gw] = st * dec_state[:, g * gw:(g + 1) * gw] + s_new
        cb = lax.dot_general(cg, bg, (((1,), (1,)), ((), ())), preferred_element_type=F32)
        for pair in range(2):
            h0 = 4 * g + 2 * pair
            ws = []
            for h in (h0, h0 + 1):
                seg = acs_col[:, h:h + 1] - acs_row[h:h + 1, :]
                ws.append(jnp.exp(jnp.where(keep, seg, -jnp.inf)) * cb * dt_row[h:h + 1, :])
            lhs = jnp.concatenate(ws, axis=1).astype(BF16)
            xp = xb[:, h0 * SSM_P:(h0 + 2) * SSM_P]
            zero = jnp.zeros_like(xp)
            rhs = jnp.concatenate([jnp.where(lane < SSM_P, xp, zero), jnp.where(lane >= SSM_P, xp, zero)], axis=0)
            y_diag = jnp.dot(lhs, rhs, preferred_element_type=F32)
            c0 = h0 * SSM_P
            y = y_off[:, pair * 128:(pair + 1) * 128] * dec_in[:, c0:c0 + 128] + y_diag
            o_ref[0, :, c0:c0 + 128] = y.astype(BF16)


def _ssd_scan(xbc, dtc, dtr, prow, pcol, expand, n_lat, reverse):
    b, lt, _ = xbc.shape
    nt = lt // CHUNK
    nl = n_lat // CHUNK
    nc = nt - nl
    if reverse:
        chunk = lambda s: nt - 1 - s
    else:
        chunk = lambda s: jnp.where(s < nc, nl + s, s - nc)
    return pl.pallas_call(
        functools.partial(_ssd_kernel, reverse=reverse),
        grid=(b, nt),
        in_specs=[
            pl.BlockSpec((1, CHUNK, XBC_W), lambda bi, s: (bi, chunk(s), 0)),
            pl.BlockSpec((1, 1, CHUNK, SSM_HEADS), lambda bi, s: (bi, chunk(s), 0, 0)),
            pl.BlockSpec((1, 1, SSM_HEADS, CHUNK), lambda bi, s: (bi, chunk(s), 0, 0)),
            pl.BlockSpec((2, SSM_HEADS), lambda bi, s: (0, 0)),
            pl.BlockSpec((SSM_HEADS, 2), lambda bi, s: (0, 0)),
            pl.BlockSpec((3 * SSM_HEADS, D), lambda bi, s: (0, 0)),
        ],
        out_specs=pl.BlockSpec((1, CHUNK, D), lambda bi, s: (bi, chunk(s), 0)),
        out_shape=jax.ShapeDtypeStruct((b, lt, D), BF16),
        scratch_shapes=[pltpu.VMEM((D_STATE, D), F32)],
        compiler_params=_cp("parallel", "arbitrary"),
    )(xbc, dtc, dtr, prow, pcol, expand)


def _rope_kernel(q_ref, k_ref, v_ref, cos_ref, sin_ref, qo_ref, ko_ref, vt_ref):
    vt_ref[0] = v_ref[0].astype(F32).T.astype(BF16)
    cos = jnp.tile(cos_ref[...], (1, D // 128))
    sin = jnp.tile(sin_ref[...], (1, D // 128))
    lane = lax.broadcasted_iota(jnp.int32, cos.shape, 1)
    first = (lane % 32) < ROPE_PAIRS

    def rot(t):
        nxt = pltpu.roll(t, D - ROPE_PAIRS, axis=1)
        prv = pltpu.roll(t, ROPE_PAIRS, axis=1)
        return t * cos + jnp.where(first, nxt, prv) * sin

    qo_ref[0] = rot(q_ref[0].astype(F32)).astype(BF16)
    ko_ref[0] = rot(k_ref[0].astype(F32)).astype(BF16)


def _rope(p, cos, sin_signed, tr):
    b, l, _ = p.shape
    spec = lambda c: pl.BlockSpec((1, tr, D), lambda bi, i: (bi, i, c))
    tab = pl.BlockSpec((tr, 128), lambda bi, i: (i, 0))
    out = pl.BlockSpec((1, tr, D), lambda bi, i: (bi, i, 0))
    return pl.pallas_call(
        _rope_kernel,
        grid=(b, l // tr),
        in_specs=[spec(COL_Q // D), spec(COL_K // D), spec(COL_V // D), tab, tab],
        out_specs=[out, out, pl.BlockSpec((1, D, tr), lambda bi, i: (bi, 0, i))],
        out_shape=[jax.ShapeDtypeStruct((b, l, D), BF16)] * 2 + [jax.ShapeDtypeStruct((b, D, l), BF16)],
        compiler_params=_cp("parallel", "parallel"),
    )(p, p, p, cos, sin_signed)


ONES_ROWS = 16


def _attn_lat_kernel(lam_ref, q_ref, k_ref, vt_ref, kc_ref, vtc_ref, wn_ref, o_ref,
                     qst_scr, m_scr, acc_scr, s_scr, *, tq, tk, scale):
    q = q_ref[0].astype(F32) * scale
    lane = lax.broadcasted_iota(jnp.int32, q.shape, 1)
    qs = jnp.concatenate([jnp.where(lane < HEAD_DIM, q, 0.0), jnp.where(lane >= HEAD_DIM, q, 0.0)], axis=0)
    qst_scr[...] = qs.T.astype(BF16)
    m_scr[...] = jnp.full_like(m_scr, -1e30)
    acc_scr[...] = jnp.zeros_like(acc_scr)
    n = k_ref.shape[1] // tk

    def qk(j, slot):
        off = pl.multiple_of(j * tk, tk)
        s_scr[slot] = jnp.dot(k_ref[0, pl.ds(off, tk), :], qst_scr[...], preferred_element_type=F32)

    def soft_pv(st, vt_blk):
        vt_ext = jnp.concatenate([vt_blk, jnp.ones((ONES_ROWS, vt_blk.shape[1]), BF16)], axis=0)
        m_prev = m_scr[...]
        m_new = jnp.maximum(m_prev, jnp.max(st, axis=0, keepdims=True))
        alpha = jnp.exp2(m_prev - m_new)
        pt = jnp.exp2((st - m_new).astype(BF16))
        acc_scr[...] = alpha * acc_scr[...] + jnp.dot(vt_ext, pt, preferred_element_type=F32)
        m_scr[...] = m_new

    def use(j, slot):
        off = pl.multiple_of(j * tk, tk)
        soft_pv(s_scr[slot], vt_ref[0, :, pl.ds(off, tk)])

    qk(0, 0)

    def body(i, carry):
        j = 2 * i
        qk(j + 1, 1)
        use(j, 0)
        qk(jnp.minimum(j + 2, n - 1), 0)
        use(j + 1, 1)
        return carry

    lax.fori_loop(0, n // 2, body, 0)
    soft_pv(jnp.dot(kc_ref[0], qst_scr[...], preferred_element_type=F32), vtc_ref[0])

    acc = acc_scr[...]
    o1 = acc[0:128, 0:tq] / acc[128:129, 0:tq]
    o2 = acc[0:128, tq:2 * tq] / acc[128:129, tq:2 * tq]
    o = (o1 - lam_ref[0] * o2).T
    o_ref[0] = (_rms(o) * wn_ref[...]).astype(BF16)


def _attention_latent(lam, q, k, vt, kc, kc_col, vtc, wn, tq, tk):
    b, sq = q.shape[0], q.shape[1]
    assert (k.shape[1] // tk) % 2 == 0
    return pl.pallas_call(
        functools.partial(_attn_lat_kernel, tq=tq, tk=tk, scale=ATTN_SCALE * LOG2E),
        grid=(b, ATTN_HEADS, sq // tq),
        in_specs=[pl.BlockSpec(memory_space=pltpu.SMEM),
                  pl.BlockSpec((1, tq, 128), lambda bi, h, i: (bi, i, h)),
                  pl.BlockSpec((1, k.shape[1], 128), lambda bi, h, i: (bi, 0, h)),
                  pl.BlockSpec((1, 128, vt.shape[2]), lambda bi, h, i: (bi, h, 0)),
                  pl.BlockSpec((1, kc.shape[1], 128), lambda bi, h, i: (bi, 0, kc_col * ATTN_HEADS + h)),
                  pl.BlockSpec((1, 128, vtc.shape[2]), lambda bi, h, i: (bi, h, 0)),
                  pl.BlockSpec((1, 128), lambda bi, h, i: (0, 0))],
        out_specs=pl.BlockSpec((1, tq, 128), lambda bi, h, i: (bi, i, h)),
        out_shape=jax.ShapeDtypeStruct((b, sq, D), BF16),
        scratch_shapes=[pltpu.VMEM((128, 2 * tq), BF16), pltpu.VMEM((1, 2 * tq), F32),
                        pltpu.VMEM((128 + ONES_ROWS, 2 * tq), F32), pltpu.VMEM((2, tk, 2 * tq), F32)],
        compiler_params=_cp("parallel", "parallel", "parallel"),
    )(lam, q, k, vt, kc, vtc, wn)


def _attn_kernel(lam_ref, q_ref, *refs, n_src, tq, tk, scale):
    kv = refs[:2 * n_src]
    wn_ref, o_ref, qs_scr, m_scr, acc_scr = refs[2 * n_src:]
    q = q_ref[0].astype(F32) * scale
    lane = lax.broadcasted_iota(jnp.int32, q.shape, 1)
    qs_scr[0:tq, :] = jnp.where(lane < HEAD_DIM, q, 0.0).astype(BF16)
    qs_scr[tq:2 * tq, :] = jnp.where(lane >= HEAD_DIM, q, 0.0).astype(BF16)
    m_scr[...] = jnp.full_like(m_scr, -1e30)
    acc_scr[...] = jnp.zeros_like(acc_scr)

    def chunk(k_blk, v_blk):
        qs = qs_scr[...]
        s = lax.dot_general(qs, k_blk, (((1,), (1,)), ((), ())), preferred_element_type=F32)
        m_prev = m_scr[...]
        m_new = jnp.maximum(m_prev, jnp.max(s, axis=-1, keepdims=True))
        alpha = jnp.exp2(m_prev - m_new)
        p = jnp.exp2((s - m_new).astype(BF16))
        v_ext = jnp.concatenate([v_blk, jnp.ones_like(v_blk)], axis=1)
        acc_scr[...] = alpha * acc_scr[...] + jnp.dot(p, v_ext, preferred_element_type=F32)
        m_scr[...] = m_new

    for si in range(n_src):
        k_ref, v_ref = kv[2 * si], kv[2 * si + 1]
        n_keys = k_ref.shape[1]
        step = min(tk, n_keys)

        def body(j, carry, k_ref=k_ref, v_ref=v_ref, step=step):
            off = pl.multiple_of(j * step, step)
            chunk(k_ref[0, pl.ds(off, step), :], v_ref[0, pl.ds(off, step), :])
            return carry

        lax.fori_loop(0, n_keys // step, body, 0)

    acc = acc_scr[...]
    o1 = acc[0:tq, 0:128] / acc[0:tq, 128:256]
    o2 = acc[tq:2 * tq, 0:128] / acc[tq:2 * tq, 128:256]
    o = o1 - lam_ref[0] * o2
    o_ref[0] = (_rms(o) * wn_ref[...]).astype(BF16)


def _attention(lam, q, q_col, kvs, wn, tq, tk):
    b, sq = q.shape[0], q.shape[1]
    in_specs = [pl.BlockSpec(memory_space=pltpu.SMEM),
                pl.BlockSpec((1, tq, 128), lambda bi, h, i, c=q_col: (bi, i, c * ATTN_HEADS + h))]
    args = [lam, q]
    for k, kc, v, vc in kvs:
        in_specs.append(pl.BlockSpec((1, k.shape[1], 128), lambda bi, h, i, c=kc: (bi, 0, c * ATTN_HEADS + h)))
        in_specs.append(pl.BlockSpec((1, v.shape[1], 128), lambda bi, h, i, c=vc: (bi, 0, c * ATTN_HEADS + h)))
        args += [k, v]
    in_specs.append(pl.BlockSpec((1, 128), lambda bi, h, i: (0, 0)))
    args.append(wn)
    return pl.pallas_call(
        functools.partial(_attn_kernel, n_src=len(kvs), tq=tq, tk=tk, scale=ATTN_SCALE * LOG2E),
        grid=(b, ATTN_HEADS, sq // tq),
        in_specs=in_specs,
        out_specs=pl.BlockSpec((1, tq, 128), lambda bi, h, i: (bi, i, h)),
        out_shape=jax.ShapeDtypeStruct((b, sq, D), BF16),
        scratch_shapes=[pltpu.VMEM((2 * tq, 128), BF16), pltpu.VMEM((2 * tq, 1), F32),
                        pltpu.VMEM((2 * tq, 256), F32)],
        compiler_params=_cp("parallel", "parallel", "parallel"),
    )(*args)


def _dft_tables(n):
    k = np.arange(n)
    ang = 2.0 * np.pi * ((k[:, None] * k[None, :]) % n) / n
    return np.cos(ang), np.sin(ang)


def _four0_kernel(x_ref, w_ref, zr_ref, zi_ref):
    y = jnp.dot(x_ref[0], w_ref[...], preferred_element_type=F32)
    zr_ref[0] = y[:, 0:FOURIER_GW].astype(BF16)
    zi_ref[0] = y[:, FOURIER_GW:2 * FOURIER_GW].astype(BF16)


def _fourier_channel(p, tm):
    b, l, _ = p.shape
    c, s = _dft_tables(FOURIER_GW)
    w0 = jnp.asarray(np.concatenate([c, s], axis=1) / math.sqrt(FOURIER_GW), BF16)
    cb = COL_F // FOURIER_GW
    out = pl.BlockSpec((1, tm, FOURIER_GW), lambda bi, i, g: (bi, i, g))
    return pl.pallas_call(
        _four0_kernel,
        grid=(b, l // tm, FOURIER_GROUPS),
        in_specs=[pl.BlockSpec((1, tm, FOURIER_GW), lambda bi, i, g: (bi, i, cb + g)),
                  pl.BlockSpec((FOURIER_GW, 2 * FOURIER_GW), lambda bi, i, g: (0, 0))],
        out_specs=[out, out],
        out_shape=[jax.ShapeDtypeStruct((b, l, D), BF16)] * 2,
        compiler_params=_cp("parallel", "parallel", "parallel"),
    )(p, w0)


SEQ_INNER = 128
SEQ_GROUP = 16


LANES = 128


def _to_lane_tiles(scr, x):
    for c in range(scr.shape[0]):
        scr[c] = x[:, c * LANES:(c + 1) * LANES]


def _from_lane_tiles(scr):
    return jnp.concatenate([scr[c] for c in range(scr.shape[0])], axis=1)


def _strided_rows(scr, rows):
    return jnp.concatenate([scr[c, rows, :] for c in range(scr.shape[0])], axis=1)


def _store_strided_rows(scr, rows, x):
    for c in range(scr.shape[0]):
        scr[c, rows, :] = x[:, c * LANES:(c + 1) * LANES]


def _four1_kernel(zr_ref, zi_ref, m_ref, c_ref, xr_scr, xi_scr, or_scr, oi_scr):
    na = zr_ref.shape[1]
    g = SEQ_GROUP
    _to_lane_tiles(xr_scr, zr_ref[0].astype(F32).reshape(na * g, D))
    _to_lane_tiles(xi_scr, zi_ref[0].astype(F32).reshape(na * g, D))
    for j in range(g):
        rows = pl.ds(j, na, stride=g)
        x = jnp.concatenate([_strided_rows(xr_scr, rows), _strided_rows(xi_scr, rows)], axis=0)
        res = jnp.dot(m_ref[j], x.astype(BF16), preferred_element_type=F32)
        _store_strided_rows(or_scr, rows, res[0:na])
        _store_strided_rows(oi_scr, rows, res[na:2 * na])
    c_ref[0, 0] = _from_lane_tiles(or_scr).reshape(na, g, D).astype(BF16)
    c_ref[0, 1] = _from_lane_tiles(oi_scr).reshape(na, g, D).astype(BF16)


def _four2_kernel(c_ref, w_ref, o_ref, o_scr):
    g = c_ref.shape[2]
    nb = SEQ_INNER
    for j in range(g):
        x = jnp.concatenate([c_ref[0, 0, j], c_ref[0, 1, j]], axis=0)
        _store_strided_rows(o_scr, pl.ds(j, nb, stride=g), jnp.dot(w_ref[...], x, preferred_element_type=F32))
    o_ref[0] = _from_lane_tiles(o_scr).reshape(nb, g, D).astype(BF16)


def _fourier_seq(zr, zi):
    b, l, _ = zr.shape
    nb, g = SEQ_INNER, SEQ_GROUP
    na = l // nb
    g2 = min(g, na)
    assert l % nb == 0 and na % g2 == 0
    ka = np.arange(na)
    ang_a = 2.0 * np.pi * ((ka[:, None] * ka[None, :]) % na) / na
    tw = 2.0 * np.pi * ((np.arange(nb)[:, None] * ka[None, :]) % l) / l
    ang = ang_a[None, :, :] + tw[:, :, None]
    cr, si = np.cos(ang) / math.sqrt(na), np.sin(ang) / math.sqrt(na)
    m = jnp.asarray(np.concatenate([np.concatenate([cr, -si], axis=2),
                                    np.concatenate([si, cr], axis=2)], axis=1), BF16)
    c2, s2 = _dft_tables(nb)
    w2 = jnp.asarray(np.concatenate([c2, -s2], axis=1) / math.sqrt(nb), BF16)

    zin = pl.BlockSpec((1, na, g, D), lambda bi, j: (bi, 0, j, 0))
    c = pl.pallas_call(
        _four1_kernel,
        grid=(b, nb // g),
        in_specs=[zin, zin, pl.BlockSpec((g, 2 * na, 2 * na), lambda bi, j: (j, 0, 0))],
        out_specs=pl.BlockSpec((1, 2, na, g, D), lambda bi, j: (bi, 0, 0, j, 0)),
        out_shape=jax.ShapeDtypeStruct((b, 2, na, nb, D), BF16),
        scratch_shapes=[pltpu.VMEM((D // LANES, na * g, LANES), F32)] * 4,
        compiler_params=_cp("parallel", "parallel"),
    )(zr.reshape(b, na, nb, D), zi.reshape(b, na, nb, D), m)
    r = pl.pallas_call(
        _four2_kernel,
        grid=(b, na // g2),
        in_specs=[pl.BlockSpec((1, 2, g2, nb, D), lambda bi, j: (bi, 0, j, 0, 0)),
                  pl.BlockSpec((nb, 2 * nb), lambda bi, j: (0, 0))],
        out_specs=pl.BlockSpec((1, nb, g2, D), lambda bi, j: (bi, 0, j, 0)),
        out_shape=jax.ShapeDtypeStruct((b, nb, na, D), BF16),
        scratch_shapes=[pltpu.VMEM((D // LANES, nb * g2, LANES), F32)],
        compiler_params=_cp("parallel", "parallel"),
    )(c, w2)
    return r.reshape(b, l, D)


def _four_direct_kernel(zr_ref, zi_ref, w_ref, o_ref):
    z = jnp.concatenate([zr_ref[0], zi_ref[0]], axis=0)
    o_ref[0] = jnp.dot(w_ref[...], z, preferred_element_type=F32).astype(BF16)


def _fourier_seq_direct(zr, zi):
    b, l, _ = zr.shape
    c, s = _dft_tables(l)
    w = jnp.asarray(np.concatenate([c, -s], axis=1) / math.sqrt(l), BF16)
    blk = pl.BlockSpec((1, l, D), lambda bi: (bi, 0, 0))
    return pl.pallas_call(
        _four_direct_kernel,
        grid=(b,),
        in_specs=[blk, blk, pl.BlockSpec((l, 2 * l), lambda bi: (0, 0))],
        out_specs=blk,
        out_shape=jax.ShapeDtypeStruct((b, l, D), BF16),
        compiler_params=_cp("parallel"),
    )(zr, zi, w)


def _merge_kernel(yf_ref, yb_ref, xs_ref, z_ref, at_ref, fo_ref, g0_ref, g1_ref, g2_ref, x_ref,
                  dsk_ref, nsw_ref, wb_ref, wo_ref, nw1_ref, gate_ref, mul2_ref, add2_ref, rwt_ref,
                  x1_ref, h2_ref, lg_ref):
    xs = xs_ref[0].astype(F32)
    z = z_ref[0].astype(F32)
    y = (yf_ref[0].astype(F32) + yb_ref[0].astype(F32) + dsk_ref[...] * xs) * (z * _sigmoid(z))
    y = (_rms(y) * nsw_ref[...]).astype(BF16)
    m = _sigmoid(g0_ref[0].astype(F32)) * jnp.dot(y, wb_ref[0], preferred_element_type=F32)
    m = m + _sigmoid(g1_ref[0].astype(F32)) * jnp.dot(at_ref[0], wb_ref[1], preferred_element_type=F32)
    m = m + _sigmoid(g2_ref[0].astype(F32)) * jnp.dot(fo_ref[0], wb_ref[2], preferred_element_type=F32)
    out = jnp.dot(m.astype(BF16), wo_ref[...], preferred_element_type=F32)
    x1 = x_ref[0] + gate_ref[0] * (_rms(out) * nw1_ref[...])
    h2 = _rms(x1) * mul2_ref[0] + add2_ref[0]
    x1_ref[0] = x1
    h2_ref[0] = _pack_pairs(h2)
    lg_ref[0] = lax.dot_general(rwt_ref[...], h2, (((1,), (1,)), ((), ())), precision=HI,
                                preferred_element_type=F32)


def _merge(yf, yb, xconv, row_off, p, attn, four, x, dsk, nsw, wb, wo, nw1, gate1, mul2, add2, rwt, tm):
    b, l, _ = x.shape
    nb = l // tm
    row = lambda c=0: pl.BlockSpec((1, tm, D), lambda bi, i, c=c: (bi, i, c))
    off = lambda: pl.BlockSpec((1, tm, D), lambda bi, i: (bi, i + row_off, 0))
    vec = pl.BlockSpec((1, D), lambda bi, i: (0, 0))
    bvec = pl.BlockSpec((1, 1, D), lambda bi, i: (bi, 0, 0))
    return pl.pallas_call(
        _merge_kernel,
        grid=(b, nb),
        in_specs=[off(), off(), off(), row(COL_Z // D), row(), row(),
                  row(COL_G // D), row(COL_G // D + 1), row(COL_G // D + 2), row(),
                  vec, vec, pl.BlockSpec((3, D, D), lambda bi, i: (0, 0, 0)),
                  pl.BlockSpec((D, D), lambda bi, i: (0, 0)), vec, bvec, bvec, bvec,
                  pl.BlockSpec((N_EXPERTS, D), lambda bi, i: (0, 0))],
        out_specs=[row(), pl.BlockSpec((1, tm, D // 2), lambda bi, i: (bi, i, 0)),
                   pl.BlockSpec((1, N_EXPERTS, tm), lambda bi, i: (bi, 0, i))],
        out_shape=[jax.ShapeDtypeStruct((b, l, D), F32), jax.ShapeDtypeStruct((b, l, D // 2), jnp.uint32),
                   jax.ShapeDtypeStruct((b, N_EXPERTS, l), F32)],
        compiler_params=_cp("parallel", "parallel"),
    )(yf, yb, xconv, p, attn, four, p, p, p, x, dsk, nsw, wb, wo, nw1, gate1, mul2, add2, rwt)


def _router_kernel(lg_ref, bias_ref, idx_ref, gate_ref, rank_ref, cnt_ref, base_scr):
    @pl.when(pl.program_id(0) == 0)
    def _():
        base_scr[...] = jnp.zeros_like(base_scr)

    lg = lg_ref[...] + bias_ref[...]
    tr = lg.shape[1]
    eid = lax.broadcasted_iota(jnp.int32, lg.shape, 0)
    work = lg
    vals, hots = [], []
    for k in range(TOP_K):
        mx = jnp.max(work, axis=0, keepdims=True)
        sel = jnp.min(jnp.where(work == mx, eid, N_EXPERTS), axis=0, keepdims=True)
        hot = eid == sel
        idx_ref[k:k + 1, :] = sel
        vals.append(mx)
        hots.append(hot)
        work = jnp.where(hot, -jnp.inf, work)
    es = [jnp.exp(v - vals[0]) for v in vals]
    den = es[0] + es[1] + es[2] + es[3]
    for k in range(TOP_K):
        gate_ref[k:k + 1, :] = es[k] / den

    multi = jnp.zeros(lg.shape, F32)
    for hot in hots:
        multi = multi + hot.astype(F32)
    r = lax.broadcasted_iota(jnp.int32, (tr, tr), 0)
    c = lax.broadcasted_iota(jnp.int32, (tr, tr), 1)
    before = (r < c).astype(BF16)
    cnt = jnp.dot(multi.astype(BF16), before, preferred_element_type=F32) + base_scr[:, 0:1]
    for k in range(TOP_K):
        rank = jnp.sum(jnp.where(hots[k], cnt, 0.0), axis=0, keepdims=True)
        rank_ref[k:k + 1, :] = rank.astype(jnp.int32)
    base_scr[...] = base_scr[...] + jnp.sum(multi, axis=1, keepdims=True)
    cnt_ref[...] = base_scr[...]


def _route(lgt, bias, tr):
    t = lgt.shape[1]
    blk = pl.BlockSpec((TOP_K, tr), lambda i: (0, i))
    return pl.pallas_call(
        _router_kernel,
        grid=(t // tr,),
        in_specs=[pl.BlockSpec((N_EXPERTS, tr), lambda i: (0, i)), pl.BlockSpec((N_EXPERTS, 1), lambda i: (0, 0))],
        out_specs=[blk, blk, blk, pl.BlockSpec((N_EXPERTS, 128), lambda i: (0, 0))],
        out_shape=[jax.ShapeDtypeStruct((TOP_K, t), jnp.int32), jax.ShapeDtypeStruct((TOP_K, t), F32),
                   jax.ShapeDtypeStruct((TOP_K, t), jnp.int32), jax.ShapeDtypeStruct((N_EXPERTS, 128), F32)],
        scratch_shapes=[pltpu.VMEM((N_EXPERTS, 128), F32)],
        compiler_params=_cp("arbitrary"),
    )(lgt, bias)


def _row_copy(src, dst, sem):
    return pltpu.make_async_copy(src, dst, sem)


def _dispatch_kernel(dest_ref, h_ref, zero_ref, xs_ref, sem):
    del zero_ref
    td = h_ref.shape[0]

    def start(t, c):
        for k in range(TOP_K):
            _row_copy(h_ref.at[pl.ds(t, 1)], xs_ref.at[pl.ds(dest_ref[k, t], 1)], sem).start()
        return c

    lax.fori_loop(0, td, start, 0)

    def wait(t, c):
        for k in range(TOP_K):
            _row_copy(h_ref.at[pl.ds(0, 1)], xs_ref.at[pl.ds(0, 1)], sem).wait()
        return c

    lax.fori_loop(0, td, wait, 0)


def _dispatch(dest, h2p, xs_init, tok_off, td):
    t = h2p.shape[0]
    ob = tok_off // td
    return pl.pallas_call(
        _dispatch_kernel,
        grid=(t // td,),
        in_specs=[pl.BlockSpec((TOP_K, td), lambda i: (0, i + ob), memory_space=pltpu.SMEM),
                  pl.BlockSpec((td, D // 2), lambda i: (i, 0)),
                  pl.BlockSpec(memory_space=pl.ANY)],
        out_specs=pl.BlockSpec(memory_space=pl.ANY),
        out_shape=jax.ShapeDtypeStruct(xs_init.shape, xs_init.dtype),
        scratch_shapes=[pltpu.SemaphoreType.DMA(())],
        input_output_aliases={2: 0},
        compiler_params=_cp("arbitrary"),
    )(dest, h2p, xs_init)


def _ffn_kernel(be_ref, nu_ref, xs_ref, w1_ref, b1_ref, w2_ref, b2_ref, ys_ref, w1b_scr, w2b_scr):
    i = pl.program_id(0)
    new_expert = jnp.logical_or(i == 0, be_ref[i] != be_ref[jnp.maximum(i - 1, 0)])

    @pl.when(new_expert)
    def _():
        w1b_scr[...] = w1_ref[0, 0].astype(BF16)
        w2b_scr[...] = w2_ref[0, 0].astype(BF16)

    @pl.when(i < nu_ref[0])
    def _():
        x = _unpack_pairs(xs_ref[...]).astype(BF16)
        gu = jnp.dot(x, w1b_scr[...], preferred_element_type=F32) + b1_ref[0, 0]
        g = jnp.minimum(gu[:, 0:D], SWIGLU_LIMIT)
        u = jnp.clip(gu[:, D:2 * D], -SWIGLU_LIMIT, SWIGLU_LIMIT)
        act = g * _sigmoid(SWIGLU_ALPHA * g) * (u + 1.0)
        y = jnp.dot(act.astype(BF16), w2b_scr[...], preferred_element_type=F32) + b2_ref[0, 0]
        ys_ref[...] = _pack_pairs(y)

    @pl.when(i >= nu_ref[0])
    def _():
        ys_ref[...] = jnp.zeros_like(ys_ref)


def _expert_ffn(block_e, n_used, xs, w1, b1, w2, b2, layer):
    n_slots = xs.shape[0]
    bm = MOE_BLOCK
    grid_spec = pltpu.PrefetchScalarGridSpec(
        num_scalar_prefetch=2,
        grid=(n_slots // bm,),
        in_specs=[
            pl.BlockSpec((bm, D // 2), lambda i, be, nu: (i, 0)),
            pl.BlockSpec((1, 1, D, 2 * D), lambda i, be, nu: (layer, be[i], 0, 0)),
            pl.BlockSpec((1, 1, 1, 2 * D), lambda i, be, nu: (layer, be[i], 0, 0)),
            pl.BlockSpec((1, 1, D, D), lambda i, be, nu: (layer, be[i], 0, 0)),
            pl.BlockSpec((1, 1, 1, D), lambda i, be, nu: (layer, be[i], 0, 0)),
        ],
        out_specs=pl.BlockSpec((bm, D // 2), lambda i, be, nu: (i, 0)),
        scratch_shapes=[pltpu.VMEM((D, 2 * D), BF16), pltpu.VMEM((D, D), BF16)],
    )
    return pl.pallas_call(
        _ffn_kernel,
        grid_spec=grid_spec,
        out_shape=jax.ShapeDtypeStruct((n_slots, D // 2), jnp.uint32),
        compiler_params=_cp("arbitrary"),
    )(block_e, n_used, xs, w1, b1, w2, b2)


def _combine_kernel(dest_ref, gate_ref, x_ref, g2_ref, nw_ref, ys_ref, o_ref, buf, sem):
    tc = x_ref.shape[0]

    def start(t, c):
        for k in range(TOP_K):
            _row_copy(ys_ref.at[pl.ds(dest_ref[k, t], 1)], buf.at[k, pl.ds(t, 1)], sem).start()
        return c

    lax.fori_loop(0, tc, start, 0)

    def wait(t, c):
        for k in range(TOP_K):
            _row_copy(ys_ref.at[pl.ds(0, 1)], buf.at[0, pl.ds(0, 1)], sem).wait()
        return c

    lax.fori_loop(0, tc, wait, 0)
    y = gate_ref[:, 0:1] * _unpack_pairs(buf[0])
    for k in range(1, TOP_K):
        y = y + gate_ref[:, k:k + 1] * _unpack_pairs(buf[k])
    o_ref[...] = x_ref[...] + g2_ref[0] * (_rms(y) * nw_ref[...])


def _combine(dest, gate_t, x1, g2_blocks, nw3, ys, tok_off, tc):
    t = x1.shape[0]
    ob = tok_off // tc
    return pl.pallas_call(
        _combine_kernel,
        grid=(t // tc,),
        in_specs=[pl.BlockSpec((TOP_K, tc), lambda i: (0, i + ob), memory_space=pltpu.SMEM),
                  pl.BlockSpec((tc, TOP_K), lambda i: (i + ob, 0)),
                  pl.BlockSpec((tc, D), lambda i: (i, 0)),
                  pl.BlockSpec((1, 1, D), lambda i: (i, 0, 0)),
                  pl.BlockSpec((1, D), lambda i: (0, 0)),
                  pl.BlockSpec(memory_space=pl.ANY)],
        out_specs=pl.BlockSpec((tc, D), lambda i: (i, 0)),
        out_shape=jax.ShapeDtypeStruct((t, D), F32),
        scratch_shapes=[pltpu.VMEM((TOP_K, tc, D // 2), jnp.uint32), pltpu.SemaphoreType.DMA(())],
        compiler_params=_cp("arbitrary"),
    )(dest, gate_t, x1, g2_blocks, nw3, ys)


def _rope_tables(seqlen):
    t = jnp.arange(seqlen)
    row = (t // GRID_W).astype(F32)
    col = (t % GRID_W).astype(F32)
    inv_freq = ROPE_BASE ** (-jnp.arange(ROPE_PAIRS, dtype=F32) / ROPE_PAIRS)
    ang_r = row[:, None] * inv_freq
    ang_c = col[:, None] * inv_freq
    ang = jnp.concatenate([ang_r, ang_r, ang_c, ang_c], axis=-1)
    ang = jnp.concatenate([ang, ang], axis=-1)
    first = (jnp.arange(128) % 32) < ROPE_PAIRS
    return jnp.cos(ang), jnp.where(first, -jnp.sin(ang), jnp.sin(ang))


def _repack_w_in(w):
    main = jnp.concatenate([w[:, :XBC_W], w[:, XBC_W + DT_W:]], axis=1).astype(BF16)
    wdt = jnp.pad(w[:, XBC_W:XBC_W + DT_W], ((0, 0), (0, 128 - DT_W))).astype(BF16)
    return main, wdt


def _split_dt(dt):
    b, l, _ = dt.shape
    d = dt[:, :, :DT_W].reshape(b, l // CHUNK, CHUNK, 2, SSM_HEADS)
    dc = jnp.moveaxis(d, 3, 0)
    return dc, jnp.swapaxes(dc, -1, -2)


def _moe(lgt, router_b, h2_parts, w1, b1, w2, b2, layer):
    t = lgt.shape[1]
    bm = MOE_BLOCK
    idx, gate, rank, cnt = _route(lgt, router_b.reshape(N_EXPERTS, 1), 256)
    counts = cnt[:, 0].astype(jnp.int32)
    padded = (counts + bm - 1) // bm * bm
    end_padded = jnp.cumsum(padded)
    start_padded = end_padded - padded
    experts = jnp.arange(N_EXPERTS, dtype=jnp.int32)[:, None, None]
    dest = rank + jnp.sum(jnp.where(idx[None] == experts, start_padded[:, None, None], 0), axis=0)
    n_slots = (t * TOP_K + bm - 1) // bm * bm + N_EXPERTS * bm
    n_blocks = n_slots // bm
    n_used = (end_padded[-1] // bm).astype(jnp.int32)
    blk = jnp.minimum(jnp.arange(n_blocks, dtype=jnp.int32), n_used - 1)
    block_e = jnp.sum((end_padded[None, :] <= (blk * bm)[:, None]).astype(jnp.int32), axis=1)
    block_e = jnp.minimum(block_e, N_EXPERTS - 1)
    xs = jnp.zeros((n_slots, D // 2), jnp.uint32)
    off = 0
    for part in h2_parts:
        xs = _dispatch(dest, part, xs, off, 128)
        off += part.shape[0]
    ys = _expert_ffn(block_e, n_used.reshape(1), xs, w1, b1[:, :, None, :], w2, b2[:, :, None, :], layer)
    return dest, gate.T, ys


def kernel(x, c, ctx, c_ctx, ada_w, ada_b, norm_w, w_in, conv_w, conv_b, dt_bias, a_log, d_skip, ssm_norm_w,
           lambda_qk, subln_w, w_branch, w_out, router_w, router_b, expert_w1, expert_b1, expert_w2, expert_b2):
    bsz, seqlen, _ = x.shape
    ctx_len = ctx.shape[1]
    depth = ada_w.shape[0]
    n_tok = bsz * seqlen
    rope_cos, rope_sin = _rope_tables(seqlen)
    expand = jnp.tile(jnp.repeat(jnp.eye(SSM_HEADS, dtype=BF16), SSM_P, axis=1), (3, 1))
    tm = min(1024, seqlen)
    tq = min(512, seqlen)

    for layer in range(depth):
        last = layer == depth - 1
        lambda_init = 0.8 - 0.6 * math.exp(-0.3 * layer)
        mod = jax.nn.silu(c) @ ada_w[layer] + ada_b[layer]
        mod_c = jax.nn.silu(c_ctx) @ ada_w[layer] + ada_b[layer]
        sh1, sc1, g1, sh2, sc2, g2 = [m[:, None, :] for m in jnp.split(mod, 6, axis=-1)]
        csh1, csc1, cg1, csh2, csc2, cg2 = [jnp.broadcast_to(m[None, None, :], (bsz, 1, D))
                                            for m in jnp.split(mod_c, 6)]
        nw = norm_w[layer]
        w_main, w_dt = _repack_w_in(w_in[layer])

        p, dt = _project(x, nw[0] * (1 + sc1), sh1, w_main, w_dt, tm)
        pc, dt_c = _project(ctx, nw[0] * (1 + csc1), csh1, w_main, w_dt, ctx_len)

        conv_wt = conv_w[layer].T
        conv_bb = conv_b[layer][None, :]
        xconv = _conv_silu(p, pc, conv_wt, conv_bb)
        dtc, dtr = _split_dt(jnp.concatenate([dt, dt_c], axis=1))
        a = -jnp.exp(a_log[layer].astype(F32))
        ys_dir = []
        for d in range(2):
            prow = jnp.stack([dt_bias[layer][d], a[d]], axis=0)
            ys_dir.append(_ssd_scan(xconv, dtc[d], dtr[d], prow, prow.T, expand, seqlen, reverse=(d == 1)))
        y_f, y_b = ys_dir

        lq = lambda_qk[layer].astype(F32)
        lam = (jnp.exp(jnp.sum(lq[0] * lq[1])) - jnp.exp(jnp.sum(lq[2] * lq[3])) + lambda_init).reshape(1)
        wn = (subln_w[layer] * (1.0 - lambda_init))[None, :]
        q_rot, k_rot, v_t = _rope(p, rope_cos, rope_sin, min(512, seqlen))
        kc, vc = COL_K // D, COL_V // D
        v_t_ctx = jnp.swapaxes(pc[:, :, COL_V:COL_V + D], 1, 2)
        y_attn = _attention_latent(lam, q_rot, k_rot, v_t, pc, kc, v_t_ctx, wn, tq, min(512, seqlen // 2))

        zr, zi = _fourier_channel(p, min(512, seqlen))
        y_four = _fourier_seq(zr, zi)

        dsk = jnp.repeat(d_skip[layer], SSM_P)[None, :]
        nsw = ssm_norm_w[layer][None, :]
        wb = w_branch[layer].astype(BF16)
        wo = w_out[layer].astype(BF16)
        rwt = router_w[layer].T
        tmm = min(256, seqlen)
        x1, h2, lgt = _merge(y_f, y_b, xconv, 0, p, y_attn, y_four, x, dsk, nsw, wb, wo, nw[1][None, :],
                             g1, nw[2] * (1 + sc2), sh2, rwt, tmm)
        h2_parts = [h2.reshape(n_tok, D // 2)]
        lgt = jnp.moveaxis(lgt, 0, 1).reshape(N_EXPERTS, n_tok)

        if not last:
            y_attn_c = _attention(lam, pc, COL_Q // D, [(pc, kc, pc, vc)], wn, ctx_len, ctx_len)
            zr_c, zi_c = _fourier_channel(pc, ctx_len)
            y_four_c = _fourier_seq_direct(zr_c, zi_c)
            ctx1, hc2, lgt_c = _merge(y_f, y_b, xconv, seqlen // ctx_len, pc, y_attn_c, y_four_c, ctx, dsk, nsw,
                                      wb, wo, nw[1][None, :], cg1, nw[2] * (1 + csc2), csh2, rwt, ctx_len)
            h2_parts.append(hc2.reshape(bsz * ctx_len, D // 2))
            lgt = jnp.concatenate([lgt, jnp.moveaxis(lgt_c, 0, 1).reshape(N_EXPERTS, bsz * ctx_len)], axis=1)

        dest, gate_t, ys = _moe(lgt, router_b[layer], h2_parts, expert_w1, expert_b1, expert_w2, expert_b2, layer)
        tc = 128
        g2_blocks = jnp.repeat(g2, seqlen // tc, axis=0)
        x = _combine(dest, gate_t, x1.reshape(n_tok, D), g2_blocks, nw[3][None, :], ys, 0, tc).reshape(x.shape)
        if not last:
            cg2_blocks = jnp.repeat(cg2, ctx_len // tc, axis=0)
            ctx = _combine(dest, gate_t, ctx1.reshape(bsz * ctx_len, D), cg2_blocks, nw[3][None, :], ys,
                           n_tok, tc).reshape(ctx.shape)
    return x
```

```python
import functools
import math

import numpy as np
import jax
import jax.numpy as jnp
from jax import lax
from jax.experimental import pallas as pl
from jax.experimental.pallas import tpu as pltpu
from jax.experimental.pallas import tpu_sc as plsc

F32 = jnp.float32
BF16 = jnp.bfloat16
HI = lax.Precision.HIGHEST

D = 1024
GRID_W = 64
EPS = 1e-6
SSM_HEADS = 16
SSM_P = 64
N_GROUPS = 4
D_STATE = 128
CHUNK = 128
XBC_W = D + 2 * N_GROUPS * D_STATE
DT_W = 2 * SSM_HEADS
HEAD_DIM = 64
ATTN_HEADS = 8
ATTN_SCALE = HEAD_DIM ** -0.5
ROPE_BASE = 10000.0
ROPE_PAIRS = 16
FOURIER_GROUPS = 4
FOURIER_GW = 256
N_EXPERTS = 32
TOP_K = 4
SWIGLU_LIMIT = 7.0
SWIGLU_ALPHA = 1.702
LOG2E = 1.4426950408889634

COL_XBC, COL_K, COL_V, COL_Z, COL_Q, COL_F, COL_G = 0, 2048, 3072, 4096, 5120, 6144, 7168
NP = 10240

VMEM_LIMIT = 52 * 1024 * 1024
MOE_BLOCK = 256


def _cp(*sem):
    return pltpu.CompilerParams(dimension_semantics=sem, vmem_limit_bytes=VMEM_LIMIT)


def _sigmoid(x):
    return 0.5 * jnp.tanh(0.5 * x) + 0.5


def _rms(x):
    return x * lax.rsqrt(jnp.mean(x * x, axis=-1, keepdims=True) + EPS)


def _pack_pairs(x):
    half = x.shape[1] // 2
    lo = lax.bitcast_convert_type(x[:, :half].astype(BF16).astype(F32), jnp.uint32)
    hi = lax.bitcast_convert_type(x[:, half:].astype(BF16).astype(F32), jnp.uint32)
    return (lo >> 16) | (hi & jnp.uint32(0xFFFF0000))


def _unpack_pairs(u):
    lo = lax.bitcast_convert_type(u << 16, F32)
    hi = lax.bitcast_convert_type(u & jnp.uint32(0xFFFF0000), F32)
    return jnp.concatenate([lo, hi], axis=1)


def _proj_kernel(x_ref, mul_ref, add_ref, w_ref, wdt_ref, o_ref, dt_ref, h_scr):
    @pl.when(pl.program_id(2) == 0)
    def _():
        h = _rms(x_ref[0]) * mul_ref[0] + add_ref[0]
        hb = h.astype(BF16)
        h_scr[...] = hb
        dt_ref[0] = jnp.dot(hb, wdt_ref[...], preferred_element_type=F32)

    o_ref[0] = jnp.dot(h_scr[...], w_ref[...], preferred_element_type=F32).astype(BF16)


def _project(x, mul, add, w, wdt, tm):
    b, l, _ = x.shape
    tn = 1024
    return pl.pallas_call(
        _proj_kernel,
        grid=(b, l // tm, NP // tn),
        in_specs=[
            pl.BlockSpec((1, tm, D), lambda bi, i, j: (bi, i, 0)),
            pl.BlockSpec((1, 1, D), lambda bi, i, j: (bi, 0, 0)),
            pl.BlockSpec((1, 1, D), lambda bi, i, j: (bi, 0, 0)),
            pl.BlockSpec((D, tn), lambda bi, i, j: (0, j)),
            pl.BlockSpec((D, 128), lambda bi, i, j: (0, 0)),
        ],
        out_specs=[
            pl.BlockSpec((1, tm, tn), lambda bi, i, j: (bi, i, j)),
            pl.BlockSpec((1, tm, 128), lambda bi, i, j: (bi, i, 0)),
        ],
        out_shape=[jax.ShapeDtypeStruct((b, l, NP), BF16), jax.ShapeDtypeStruct((b, l, 128), F32)],
        scratch_shapes=[pltpu.VMEM((tm, D), BF16)],
        compiler_params=_cp("parallel", "parallel", "arbitrary"),
    )(x, mul, add, w, wdt)


def _conv_kernel(x_ref, prev_ref, next_ref, xc_ref, w_ref, b_ref, o_ref, ext_scr):
    i = pl.program_id(1)
    nl = pl.num_programs(1) - 1
    tr = x_ref.shape[1]
    is_ctx = i == nl
    prev = prev_ref[0].astype(F32)[8:16]
    nxt = next_ref[0].astype(F32)[0:8]
    ext_scr[0:8, :] = jnp.where(jnp.logical_and(i > 0, i < nl), prev, 0.0)
    ext_scr[8:8 + tr, :] = jnp.where(is_ctx, xc_ref[0], x_ref[0]).astype(F32)
    ext_scr[8 + tr:16 + tr, :] = jnp.where(i < nl - 1, nxt, 0.0)
    acc = b_ref[...] + w_ref[0:1, :] * ext_scr[6:6 + tr, :]
    for k in range(1, 5):
        acc = acc + w_ref[k:k + 1, :] * ext_scr[6 + k:6 + k + tr, :]
    o_ref[0] = (acc * _sigmoid(acc)).astype(BF16)


def _conv_silu(p, pc, conv_wt, conv_b):
    b, l, _ = p.shape
    tr = pc.shape[1]
    assert l % tr == 0 and tr % 16 == 0
    nl = l // tr
    nh = tr // 16
    last = l // 16 - 1
    return pl.pallas_call(
        _conv_kernel,
        grid=(b, nl + 1),
        in_specs=[
            pl.BlockSpec((1, tr, XBC_W), lambda bi, i: (bi, jnp.minimum(i, nl - 1), 0)),
            pl.BlockSpec((1, 16, XBC_W), lambda bi, i: (bi, jnp.clip(i * nh - 1, 0, last), 0)),
            pl.BlockSpec((1, 16, XBC_W), lambda bi, i: (bi, jnp.minimum((i + 1) * nh, last), 0)),
            pl.BlockSpec((1, tr, XBC_W), lambda bi, i: (bi, 0, 0)),
            pl.BlockSpec((5, XBC_W), lambda bi, i: (0, 0)),
            pl.BlockSpec((1, XBC_W), lambda bi, i: (0, 0)),
        ],
        out_specs=pl.BlockSpec((1, tr, XBC_W), lambda bi, i: (bi, i, 0)),
        out_shape=jax.ShapeDtypeStruct((b, l + tr, XBC_W), BF16),
        scratch_shapes=[pltpu.VMEM((tr + 16, XBC_W), F32)],
        compiler_params=_cp("parallel", "parallel"),
    )(p, p, p, pc, conv_wt, conv_b)


def _softplus(x):
    return jnp.maximum(x, 0.0) + jnp.log1p(jnp.exp(-jnp.abs(x)))


def _split3(v):
    hi = v.astype(BF16)
    r = v - hi.astype(F32)
    mid = r.astype(BF16)
    lo = (r - mid.astype(F32)).astype(BF16)
    return hi, mid, lo


def _ssd_kernel(xbc_ref, dtc_ref, dtr_ref, prow_ref, pcol_ref, e_ref, o_ref, state_scr, *, reverse):
    @pl.when(pl.program_id(1) == 0)
    def _():
        state_scr[...] = jnp.zeros_like(state_scr)

    q = CHUNK
    row = lax.broadcasted_iota(jnp.int32, (q, q), 0)
    col = lax.broadcasted_iota(jnp.int32, (q, q), 1)
    keep = (col >= row) if reverse else (col <= row)
    tri = keep.astype(BF16)
    tri_t = ((row >= col) if reverse else (row <= col)).astype(BF16)
    nh = SSM_HEADS

    dt_col = _softplus(dtc_ref[0, 0] + prow_ref[0:1, :])
    dt_row = _softplus(dtr_ref[0, 0] + pcol_ref[:, 0:1])
    adt_col = dt_col * prow_ref[1:2, :]
    adt_row = dt_row * pcol_ref[:, 1:2]
    c3 = jnp.dot(tri, jnp.concatenate(_split3(adt_col), axis=1), preferred_element_type=F32)
    acs_col = c3[:, 0:nh] + c3[:, nh:2 * nh] + c3[:, 2 * nh:3 * nh]
    r3 = jnp.dot(jnp.concatenate(_split3(adt_row), axis=0), tri_t, preferred_element_type=F32)
    acs_row = r3[0:nh] + r3[nh:2 * nh] + r3[2 * nh:3 * nh]
    a_tot = acs_col[0:1, :] if reverse else acs_col[q - 1:q, :]

    expand3 = e_ref[...]

    def expand(v):
        return jnp.dot(jnp.concatenate(_split3(v), axis=1), expand3, preferred_element_type=F32)

    wend = expand(jnp.exp(a_tot - acs_col) * dt_col)
    dec_in = expand(jnp.exp(acs_col))
    dec_state = dec_in[0:1, :] if reverse else dec_in[q - 1:q, :]

    xb = xbc_ref[0, :, 0:D]
    xw = (xb.astype(F32) * wend).astype(BF16)
    lane = lax.broadcasted_iota(jnp.int32, (q, 2 * SSM_P), 1)
    gw = (SSM_HEADS // N_GROUPS) * SSM_P
    for g in range(N_GROUPS):
        bg = xbc_ref[0, :, D + g * D_STATE:D + (g + 1) * D_STATE]
        cg = xbc_ref[0, :, D + N_GROUPS * D_STATE + g * D_STATE:D + N_GROUPS * D_STATE + (g + 1) * D_STATE]
        st = state_scr[:, g * gw:(g + 1) * gw]
        y_off = jnp.dot(cg, st.astype(BF16), preferred_element_type=F32)
        s_new = lax.dot_general(bg, xw[:, g * gw:(g + 1) * gw], (((0,), (0,)), ((), ())),
                                preferred_element_type=F32)
        state_scr[:, g * gw:(g + 1) * gw] = st * dec_state[:, g * gw:(g + 1) * gw] + s_new
        cb = lax.dot_general(cg, bg, (((1,), (1,)), ((), ())), preferred_element_type=F32)
        for pair in range(2):
            h0 = 4 * g + 2 * pair
            ws = []
            for h in (h0, h0 + 1):
                seg = acs_col[:, h:h + 1] - acs_row[h:h + 1, :]
                ws.append(jnp.exp(jnp.where(keep, seg, -jnp.inf)) * cb * dt_row[h:h + 1, :])
            lhs = jnp.concatenate(ws, axis=1).astype(BF16)
            xp = xb[:, h0 * SSM_P:(h0 + 2) * SSM_P]
            zero = jnp.zeros_like(xp)
            rhs = jnp.concatenate([jnp.where(lane < SSM_P, xp, zero), jnp.where(lane >= SSM_P, xp, zero)], axis=0)
            y_diag = jnp.dot(lhs, rhs, preferred_element_type=F32)
            c0 = h0 * SSM_P
            y = y_off[:, pair * 128:(pair + 1) * 128] * dec_in[:, c0:c0 + 128] + y_diag
            o_ref[0, :, c0:c0 + 128] = y.astype(BF16)


def _ssd_scan(xbc, dtc, dtr, prow, pcol, expand, n_lat, reverse):
    b, lt, _ = xbc.shape
    nt = lt // CHUNK
    nl = n_lat // CHUNK
    nc = nt - nl
    if reverse:
        chunk = lambda s: nt - 1 - s
    else:
        chunk = lambda s: jnp.where(s < nc, nl + s, s - nc)
    return pl.pallas_call(
        functools.partial(_ssd_kernel, reverse=reverse),
        grid=(b, nt),
        in_specs=[
            pl.BlockSpec((1, CHUNK, XBC_W), lambda bi, s: (bi, chunk(s), 0)),
            pl.BlockSpec((1, 1, CHUNK, SSM_HEADS), lambda bi, s: (bi, chunk(s), 0, 0)),
            pl.BlockSpec((1, 1, SSM_HEADS, CHUNK), lambda bi, s: (bi, chunk(s), 0, 0)),
            pl.BlockSpec((2, SSM_HEADS), lambda bi, s: (0, 0)),
            pl.BlockSpec((SSM_HEADS, 2), lambda bi, s: (0, 0)),
            pl.BlockSpec((3 * SSM_HEADS, D), lambda bi, s: (0, 0)),
        ],
        out_specs=pl.BlockSpec((1, CHUNK, D), lambda bi, s: (bi, chunk(s), 0)),
        out_shape=jax.ShapeDtypeStruct((b, lt, D), BF16),
        scratch_shapes=[pltpu.VMEM((D_STATE, D), F32)],
        compiler_params=_cp("parallel", "arbitrary"),
    )(xbc, dtc, dtr, prow, pcol, expand)


def _rope_kernel(q_ref, k_ref, v_ref, cos_ref, sin_ref, qo_ref, ko_ref, vt_ref):
    vt_ref[0] = v_ref[0].astype(F32).T.astype(BF16)
    cos = jnp.tile(cos_ref[...], (1, D // 128))
    sin = jnp.tile(sin_ref[...], (1, D // 128))
    lane = lax.broadcasted_iota(jnp.int32, cos.shape, 1)
    first = (lane % 32) < ROPE_PAIRS

    def rot(t):
        nxt = pltpu.roll(t, D - ROPE_PAIRS, axis=1)
        prv = pltpu.roll(t, ROPE_PAIRS, axis=1)
        return t * cos + jnp.where(first, nxt, prv) * sin

    qo_ref[0] = rot(q_ref[0].astype(F32)).astype(BF16)
    ko_ref[0] = rot(k_ref[0].astype(F32)).astype(BF16)


def _rope(p, cos, sin_signed, tr):
    b, l, _ = p.shape
    spec = lambda c: pl.BlockSpec((1, tr, D), lambda bi, i: (bi, i, c))
    tab = pl.BlockSpec((tr, 128), lambda bi, i: (i, 0))
    out = pl.BlockSpec((1, tr, D), lambda bi, i: (bi, i, 0))
    return pl.pallas_call(
        _rope_kernel,
        grid=(b, l // tr),
        in_specs=[spec(COL_Q // D), spec(COL_K // D), spec(COL_V // D), tab, tab],
        out_specs=[out, out, pl.BlockSpec((1, D, tr), lambda bi, i: (bi, 0, i))],
        out_shape=[jax.ShapeDtypeStruct((b, l, D), BF16)] * 2 + [jax.ShapeDtypeStruct((b, D, l), BF16)],
        compiler_params=_cp("parallel", "parallel"),
    )(p, p, p, cos, sin_signed)


ONES_ROWS = 16


def _attn_lat_kernel(lam_ref, q_ref, k_ref, vt_ref, kc_ref, vtc_ref, wn_ref, o_ref,
                     qst_scr, m_scr, acc_scr, s_scr, *, tq, tk, scale):
    q = q_ref[0].astype(F32) * scale
    lane = lax.broadcasted_iota(jnp.int32, q.shape, 1)
    qs = jnp.concatenate([jnp.where(lane < HEAD_DIM, q, 0.0), jnp.where(lane >= HEAD_DIM, q, 0.0)], axis=0)
    qst_scr[...] = qs.T.astype(BF16)
    m_scr[...] = jnp.full_like(m_scr, -1e30)
    acc_scr[...] = jnp.zeros_like(acc_scr)
    n = k_ref.shape[1] // tk

    def qk(j, slot):
        off = pl.multiple_of(j * tk, tk)
        s_scr[slot] = jnp.dot(k_ref[0, pl.ds(off, tk), :], qst_scr[...], preferred_element_type=F32)

    def soft_pv(st, vt_blk):
        vt_ext = jnp.concatenate([vt_blk, jnp.ones((ONES_ROWS, vt_blk.shape[1]), BF16)], axis=0)
        m_prev = m_scr[...]
        m_new = jnp.maximum(m_prev, jnp.max(st, axis=0, keepdims=True))
        alpha = jnp.exp2(m_prev - m_new)
        pt = jnp.exp2((st - m_new).astype(BF16))
        acc_scr[...] = alpha * acc_scr[...] + jnp.dot(vt_ext, pt, preferred_element_type=F32)
        m_scr[...] = m_new

    def use(j, slot):
        off = pl.multiple_of(j * tk, tk)
        soft_pv(s_scr[slot], vt_ref[0, :, pl.ds(off, tk)])

    qk(0, 0)

    def body(i, carry):
        j = 2 * i
        qk(j + 1, 1)
        use(j, 0)
        qk(jnp.minimum(j + 2, n - 1), 0)
        use(j + 1, 1)
        return carry

    lax.fori_loop(0, n // 2, body, 0)
    soft_pv(jnp.dot(kc_ref[0], qst_scr[...], preferred_element_type=F32), vtc_ref[0])

    acc = acc_scr[...]
    o1 = acc[0:128, 0:tq] / acc[128:129, 0:tq]
    o2 = acc[0:128, tq:2 * tq] / acc[128:129, tq:2 * tq]
    o = (o1 - lam_ref[0] * o2).T
    o_ref[0] = (_rms(o) * wn_ref[...]).astype(BF16)


def _attention_latent(lam, q, k, vt, kc, kc_col, vtc, wn, tq, tk):
    b, sq = q.shape[0], q.shape[1]
    assert (k.shape[1] // tk) % 2 == 0
    return pl.pallas_call(
        functools.partial(_attn_lat_kernel, tq=tq, tk=tk, scale=ATTN_SCALE * LOG2E),
        grid=(b, ATTN_HEADS, sq // tq),
        in_specs=[pl.BlockSpec(memory_space=pltpu.SMEM),
                  pl.BlockSpec((1, tq, 128), lambda bi, h, i: (bi, i, h)),
                  pl.BlockSpec((1, k.shape[1], 128), lambda bi, h, i: (bi, 0, h)),
                  pl.BlockSpec((1, 128, vt.shape[2]), lambda bi, h, i: (bi, h, 0)),
                  pl.BlockSpec((1, kc.shape[1], 128), lambda bi, h, i: (bi, 0, kc_col * ATTN_HEADS + h)),
                  pl.BlockSpec((1, 128, vtc.shape[2]), lambda bi, h, i: (bi, h, 0)),
                  pl.BlockSpec((1, 128), lambda bi, h, i: (0, 0))],
        out_specs=pl.BlockSpec((1, tq, 128), lambda bi, h, i: (bi, i, h)),
        out_shape=jax.ShapeDtypeStruct((b, sq, D), BF16),
        scratch_shapes=[pltpu.VMEM((128, 2 * tq), BF16), pltpu.VMEM((1, 2 * tq), F32),
                        pltpu.VMEM((128 + ONES_ROWS, 2 * tq), F32), pltpu.VMEM((2, tk, 2 * tq), F32)],
        compiler_params=_cp("parallel", "parallel", "parallel"),
    )(lam, q, k, vt, kc, vtc, wn)


def _attn_kernel(lam_ref, q_ref, *refs, n_src, tq, tk, scale):
    kv = refs[:2 * n_src]
    wn_ref, o_ref, qs_scr, m_scr, acc_scr = refs[2 * n_src:]
    q = q_ref[0].astype(F32) * scale
    lane = lax.broadcasted_iota(jnp.int32, q.shape, 1)
    qs_scr[0:tq, :] = jnp.where(lane < HEAD_DIM, q, 0.0).astype(BF16)
    qs_scr[tq:2 * tq, :] = jnp.where(lane >= HEAD_DIM, q, 0.0).astype(BF16)
    m_scr[...] = jnp.full_like(m_scr, -1e30)
    acc_scr[...] = jnp.zeros_like(acc_scr)

    def chunk(k_blk, v_blk):
        qs = qs_scr[...]
        s = lax.dot_general(qs, k_blk, (((1,), (1,)), ((), ())), preferred_element_type=F32)
        m_prev = m_scr[...]
        m_new = jnp.maximum(m_prev, jnp.max(s, axis=-1, keepdims=True))
        alpha = jnp.exp2(m_prev - m_new)
        p = jnp.exp2((s - m_new).astype(BF16))
        v_ext = jnp.concatenate([v_blk, jnp.ones_like(v_blk)], axis=1)
        acc_scr[...] = alpha * acc_scr[...] + jnp.dot(p, v_ext, preferred_element_type=F32)
        m_scr[...] = m_new

    for si in range(n_src):
        k_ref, v_ref = kv[2 * si], kv[2 * si + 1]
        n_keys = k_ref.shape[1]
        step = min(tk, n_keys)

        def body(j, carry, k_ref=k_ref, v_ref=v_ref, step=step):
            off = pl.multiple_of(j * step, step)
            chunk(k_ref[0, pl.ds(off, step), :], v_ref[0, pl.ds(off, step), :])
            return carry

        lax.fori_loop(0, n_keys // step, body, 0)

    acc = acc_scr[...]
    o1 = acc[0:tq, 0:128] / acc[0:tq, 128:256]
    o2 = acc[tq:2 * tq, 0:128] / acc[tq:2 * tq, 128:256]
    o = o1 - lam_ref[0] * o2
    o_ref[0] = (_rms(o) * wn_ref[...]).astype(BF16)


def _attention(lam, q, q_col, kvs, wn, tq, tk):
    b, sq = q.shape[0], q.shape[1]
    in_specs = [pl.BlockSpec(memory_space=pltpu.SMEM),
                pl.BlockSpec((1, tq, 128), lambda bi, h, i, c=q_col: (bi, i, c * ATTN_HEADS + h))]
    args = [lam, q]
    for k, kc, v, vc in kvs:
        in_specs.append(pl.BlockSpec((1, k.shape[1], 128), lambda bi, h, i, c=kc: (bi, 0, c * ATTN_HEADS + h)))
        in_specs.append(pl.BlockSpec((1, v.shape[1], 128), lambda bi, h, i, c=vc: (bi, 0, c * ATTN_HEADS + h)))
        args += [k, v]
    in_specs.append(pl.BlockSpec((1, 128), lambda bi, h, i: (0, 0)))
    args.append(wn)
    return pl.pallas_call(
        functools.partial(_attn_kernel, n_src=len(kvs), tq=tq, tk=tk, scale=ATTN_SCALE * LOG2E),
        grid=(b, ATTN_HEADS, sq // tq),
        in_specs=in_specs,
        out_specs=pl.BlockSpec((1, tq, 128), lambda bi, h, i: (bi, i, h)),
        out_shape=jax.ShapeDtypeStruct((b, sq, D), BF16),
        scratch_shapes=[pltpu.VMEM((2 * tq, 128), BF16), pltpu.VMEM((2 * tq, 1), F32),
                        pltpu.VMEM((2 * tq, 256), F32)],
        compiler_params=_cp("parallel", "parallel", "parallel"),
    )(*args)


def _dft_tables(n):
    k = np.arange(n)
    ang = 2.0 * np.pi * ((k[:, None] * k[None, :]) % n) / n
    return np.cos(ang), np.sin(ang)


def _four0_kernel(x_ref, w_ref, zr_ref, zi_ref):
    y = jnp.dot(x_ref[0], w_ref[...], preferred_element_type=F32)
    zr_ref[0] = y[:, 0:FOURIER_GW].astype(BF16)
    zi_ref[0] = y[:, FOURIER_GW:2 * FOURIER_GW].astype(BF16)


def _fourier_channel(p, tm):
    b, l, _ = p.shape
    c, s = _dft_tables(FOURIER_GW)
    w0 = jnp.asarray(np.concatenate([c, s], axis=1) / math.sqrt(FOURIER_GW), BF16)
    cb = COL_F // FOURIER_GW
    out = pl.BlockSpec((1, tm, FOURIER_GW), lambda bi, i, g: (bi, i, g))
    return pl.pallas_call(
        _four0_kernel,
        grid=(b, l // tm, FOURIER_GROUPS),
        in_specs=[pl.BlockSpec((1, tm, FOURIER_GW), lambda bi, i, g: (bi, i, cb + g)),
                  pl.BlockSpec((FOURIER_GW, 2 * FOURIER_GW), lambda bi, i, g: (0, 0))],
        out_specs=[out, out],
        out_shape=[jax.ShapeDtypeStruct((b, l, D), BF16)] * 2,
        compiler_params=_cp("parallel", "parallel", "parallel"),
    )(p, w0)


SEQ_INNER = 128
SEQ_GROUP = 16


LANES = 128


def _to_lane_tiles(scr, x):
    for c in range(scr.shape[0]):
        scr[c] = x[:, c * LANES:(c + 1) * LANES]


def _from_lane_tiles(scr):
    return jnp.concatenate([scr[c] for c in range(scr.shape[0])], axis=1)


def _strided_rows(scr, rows):
    return jnp.concatenate([scr[c, rows, :] for c in range(scr.shape[0])], axis=1)


def _store_strided_rows(scr, rows, x):
    for c in range(scr.shape[0]):
        scr[c, rows, :] = x[:, c * LANES:(c + 1) * LANES]


def _four1_kernel(zr_ref, zi_ref, m_ref, c_ref, xr_scr, xi_scr, or_scr, oi_scr):
    na = zr_ref.shape[1]
    g = SEQ_GROUP
    _to_lane_tiles(xr_scr, zr_ref[0].astype(F32).reshape(na * g, D))
    _to_lane_tiles(xi_scr, zi_ref[0].astype(F32).reshape(na * g, D))
    for j in range(g):
        rows = pl.ds(j, na, stride=g)
        x = jnp.concatenate([_strided_rows(xr_scr, rows), _strided_rows(xi_scr, rows)], axis=0)
        res = jnp.dot(m_ref[j], x.astype(BF16), preferred_element_type=F32)
        _store_strided_rows(or_scr, rows, res[0:na])
        _store_strided_rows(oi_scr, rows, res[na:2 * na])
    c_ref[0, 0] = _from_lane_tiles(or_scr).reshape(na, g, D).astype(BF16)
    c_ref[0, 1] = _from_lane_tiles(oi_scr).reshape(na, g, D).astype(BF16)


def _four2_kernel(c_ref, w_ref, o_ref, o_scr):
    g = c_ref.shape[2]
    nb = SEQ_INNER
    for j in range(g):
        x = jnp.concatenate([c_ref[0, 0, j], c_ref[0, 1, j]], axis=0)
        _store_strided_rows(o_scr, pl.ds(j, nb, stride=g), jnp.dot(w_ref[...], x, preferred_element_type=F32))
    o_ref[0] = _from_lane_tiles(o_scr).reshape(nb, g, D).astype(BF16)


def _fourier_seq(zr, zi):
    b, l, _ = zr.shape
    nb, g = SEQ_INNER, SEQ_GROUP
    na = l // nb
    g2 = min(g, na)
    assert l % nb == 0 and na % g2 == 0
    ka = np.arange(na)
    ang_a = 2.0 * np.pi * ((ka[:, None] * ka[None, :]) % na) / na
    tw = 2.0 * np.pi * ((np.arange(nb)[:, None] * ka[None, :]) % l) / l
    ang = ang_a[None, :, :] + tw[:, :, None]
    cr, si = np.cos(ang) / math.sqrt(na), np.sin(ang) / math.sqrt(na)
    m = jnp.asarray(np.concatenate([np.concatenate([cr, -si], axis=2),
                                    np.concatenate([si, cr], axis=2)], axis=1), BF16)
    c2, s2 = _dft_tables(nb)
    w2 = jnp.asarray(np.concatenate([c2, -s2], axis=1) / math.sqrt(nb), BF16)

    zin = pl.BlockSpec((1, na, g, D), lambda bi, j: (bi, 0, j, 0))
    c = pl.pallas_call(
        _four1_kernel,
        grid=(b, nb // g),
        in_specs=[zin, zin, pl.BlockSpec((g, 2 * na, 2 * na), lambda bi, j: (j, 0, 0))],
        out_specs=pl.BlockSpec((1, 2, na, g, D), lambda bi, j: (bi, 0, 0, j, 0)),
        out_shape=jax.ShapeDtypeStruct((b, 2, na, nb, D), BF16),
        scratch_shapes=[pltpu.VMEM((D // LANES, na * g, LANES), F32)] * 4,
        compiler_params=_cp("parallel", "parallel"),
    )(zr.reshape(b, na, nb, D), zi.reshape(b, na, nb, D), m)
    r = pl.pallas_call(
        _four2_kernel,
        grid=(b, na // g2),
        in_specs=[pl.BlockSpec((1, 2, g2, nb, D), lambda bi, j: (bi, 0, j, 0, 0)),
                  pl.BlockSpec((nb, 2 * nb), lambda bi, j: (0, 0))],
        out_specs=pl.BlockSpec((1, nb, g2, D), lambda bi, j: (bi, 0, j, 0)),
        out_shape=jax.ShapeDtypeStruct((b, nb, na, D), BF16),
        scratch_shapes=[pltpu.VMEM((D // LANES, nb * g2, LANES), F32)],
        compiler_params=_cp("parallel", "parallel"),
    )(c, w2)
    return r.reshape(b, l, D)


def _four_direct_kernel(zr_ref, zi_ref, w_ref, o_ref):
    z = jnp.concatenate([zr_ref[0], zi_ref[0]], axis=0)
    o_ref[0] = jnp.dot(w_ref[...], z, preferred_element_type=F32).astype(BF16)


def _fourier_seq_direct(zr, zi):
    b, l, _ = zr.shape
    c, s = _dft_tables(l)
    w = jnp.asarray(np.concatenate([c, -s], axis=1) / math.sqrt(l), BF16)
    blk = pl.BlockSpec((1, l, D), lambda bi: (bi, 0, 0))
    return pl.pallas_call(
        _four_direct_kernel,
        grid=(b,),
        in_specs=[blk, blk, pl.BlockSpec((l, 2 * l), lambda bi: (0, 0))],
        out_specs=blk,
        out_shape=jax.ShapeDtypeStruct((b, l, D), BF16),
        compiler_params=_cp("parallel"),
    )(zr, zi, w)


def _merge_kernel(yf_ref, yb_ref, xs_ref, z_ref, at_ref, fo_ref, g0_ref, g1_ref, g2_ref, x_ref,
                  dsk_ref, nsw_ref, wb_ref, wo_ref, nw1_ref, gate_ref, mul2_ref, add2_ref, rwt_ref,
                  x1_ref, h2_ref, lg_ref):
    xs = xs_ref[0].astype(F32)
    z = z_ref[0].astype(F32)
    y = (yf_ref[0].astype(F32) + yb_ref[0].astype(F32) + dsk_ref[...] * xs) * (z * _sigmoid(z))
    y = (_rms(y) * nsw_ref[...]).astype(BF16)
    m = _sigmoid(g0_ref[0].astype(F32)) * jnp.dot(y, wb_ref[0], preferred_element_type=F32)
    m = m + _sigmoid(g1_ref[0].astype(F32)) * jnp.dot(at_ref[0], wb_ref[1], preferred_element_type=F32)
    m = m + _sigmoid(g2_ref[0].astype(F32)) * jnp.dot(fo_ref[0], wb_ref[2], preferred_element_type=F32)
    out = jnp.dot(m.astype(BF16), wo_ref[...], preferred_element_type=F32)
    x1 = x_ref[0] + gate_ref[0] * (_rms(out) * nw1_ref[...])
    h2 = _rms(x1) * mul2_ref[0] + add2_ref[0]
    x1_ref[0] = x1
    h2_ref[0] = _pack_pairs(h2)
    lg_ref[0] = lax.dot_general(rwt_ref[...], h2, (((1,), (1,)), ((), ())), precision=HI,
                                preferred_element_type=F32)


def _merge(yf, yb, xconv, row_off, p, attn, four, x, dsk, nsw, wb, wo, nw1, gate1, mul2, add2, rwt, tm):
    b, l, _ = x.shape
    nb = l // tm
    row = lambda c=0: pl.BlockSpec((1, tm, D), lambda bi, i, c=c: (bi, i, c))
    off = lambda: pl.BlockSpec((1, tm, D), lambda bi, i: (bi, i + row_off, 0))
    vec = pl.BlockSpec((1, D), lambda bi, i: (0, 0))
    bvec = pl.BlockSpec((1, 1, D), lambda bi, i: (bi, 0, 0))
    return pl.pallas_call(
        _merge_kernel,
        grid=(b, nb),
        in_specs=[off(), off(), off(), row(COL_Z // D), row(), row(),
                  row(COL_G // D), row(COL_G // D + 1), row(COL_G // D + 2), row(),
                  vec, vec, pl.BlockSpec((3, D, D), lambda bi, i: (0, 0, 0)),
                  pl.BlockSpec((D, D), lambda bi, i: (0, 0)), vec, bvec, bvec, bvec,
                  pl.BlockSpec((N_EXPERTS, D), lambda bi, i: (0, 0))],
        out_specs=[row(), pl.BlockSpec((1, tm, D // 2), lambda bi, i: (bi, i, 0)),
                   pl.BlockSpec((1, N_EXPERTS, tm), lambda bi, i: (bi, 0, i))],
        out_shape=[jax.ShapeDtypeStruct((b, l, D), F32), jax.ShapeDtypeStruct((b, l, D // 2), jnp.uint32),
                   jax.ShapeDtypeStruct((b, N_EXPERTS, l), F32)],
        compiler_params=_cp("parallel", "parallel"),
    )(yf, yb, xconv, p, attn, four, p, p, p, x, dsk, nsw, wb, wo, nw1, gate1, mul2, add2, rwt)


def _router_kernel(lg_ref, bias_ref, idx_ref, gate_ref, rank_ref, cnt_ref, base_scr):
    @pl.when(pl.program_id(0) == 0)
    def _():
        base_scr[...] = jnp.zeros_like(base_scr)

    lg = lg_ref[...] + bias_ref[...]
    tr = lg.shape[1]
    eid = lax.broadcasted_iota(jnp.int32, lg.shape, 0)
    work = lg
    vals, hots = [], []
    for k in range(TOP_K):
        mx = jnp.max(work, axis=0, keepdims=True)
        sel = jnp.min(jnp.where(work == mx, eid, N_EXPERTS), axis=0, keepdims=True)
        hot = eid == sel
        idx_ref[k:k + 1, :] = sel
        vals.append(mx)
        hots.append(hot)
        work = jnp.where(hot, -jnp.inf, work)
    es = [jnp.exp(v - vals[0]) for v in vals]
    den = es[0] + es[1] + es[2] + es[3]
    for k in range(TOP_K):
        gate_ref[k:k + 1, :] = es[k] / den

    multi = jnp.zeros(lg.shape, F32)
    for hot in hots:
        multi = multi + hot.astype(F32)
    r = lax.broadcasted_iota(jnp.int32, (tr, tr), 0)
    c = lax.broadcasted_iota(jnp.int32, (tr, tr), 1)
    before = (r < c).astype(BF16)
    cnt = jnp.dot(multi.astype(BF16), before, preferred_element_type=F32) + base_scr[:, 0:1]
    for k in range(TOP_K):
        rank = jnp.sum(jnp.where(hots[k], cnt, 0.0), axis=0, keepdims=True)
        rank_ref[k:k + 1, :] = rank.astype(jnp.int32)
    base_scr[...] = base_scr[...] + jnp.sum(multi, axis=1, keepdims=True)
    cnt_ref[...] = base_scr[...]


def _route(lgt, bias, tr):
    t = lgt.shape[1]
    blk = pl.BlockSpec((TOP_K, tr), lambda i: (0, i))
    return pl.pallas_call(
        _router_kernel,
        grid=(t // tr,),
        in_specs=[pl.BlockSpec((N_EXPERTS, tr), lambda i: (0, i)), pl.BlockSpec((N_EXPERTS, 1), lambda i: (0, 0))],
        out_specs=[blk, blk, blk, pl.BlockSpec((N_EXPERTS, 128), lambda i: (0, 0))],
        out_shape=[jax.ShapeDtypeStruct((TOP_K, t), jnp.int32), jax.ShapeDtypeStruct((TOP_K, t), F32),
                   jax.ShapeDtypeStruct((TOP_K, t), jnp.int32), jax.ShapeDtypeStruct((N_EXPERTS, 128), F32)],
        scratch_shapes=[pltpu.VMEM((N_EXPERTS, 128), F32)],
        compiler_params=_cp("arbitrary"),
    )(lgt, bias)


SC_CORES = 2
SC_SUBCORES = 16
SC_MAX_ROWS = 128


def _sc_chunk(per_worker):
    for c in range(SC_MAX_ROWS, 7, -8):
        if per_worker % c == 0:
            return c
    raise ValueError(per_worker)


def _sc_scatter_rows(rows, dest, n_slots):
    t, w = rows.shape
    top_k = dest.shape[0] // t
    workers = SC_CORES * SC_SUBCORES
    assert t % workers == 0
    per_worker = t // workers
    chunk = _sc_chunk(per_worker)
    mesh = plsc.VectorSubcoreMesh(core_axis_name="c", subcore_axis_name="s")

    @functools.partial(
        pl.kernel, mesh=mesh,
        out_type=jax.ShapeDtypeStruct((n_slots, w), rows.dtype),
        scratch_types=[pltpu.VMEM((chunk,), jnp.int32), pltpu.VMEM((chunk, w), rows.dtype),
                       pltpu.SemaphoreType.DMA],
    )
    def scatter(rows_hbm, dest_hbm, out_hbm, idx_v, rows_v, sem):
        base = (lax.axis_index("s") * SC_CORES + lax.axis_index("c")) * per_worker

        @pl.loop(0, per_worker // chunk)
        def _(ci):
            off = base + ci * chunk
            pltpu.sync_copy(rows_hbm.at[pl.ds(off, chunk)], rows_v)
            for k in range(top_k):
                pltpu.sync_copy(dest_hbm.at[pl.ds(k * t + off, chunk)], idx_v)
                pltpu.async_copy(rows_v, out_hbm.at[idx_v], sem).wait()

    return scatter(rows, dest)


def _sc_gather_rows(table, idx):
    n, w = idx.shape[0], table.shape[1]
    workers = SC_CORES * SC_SUBCORES
    assert n % workers == 0
    per_worker = n // workers
    chunk = _sc_chunk(per_worker)
    mesh = plsc.VectorSubcoreMesh(core_axis_name="c", subcore_axis_name="s")

    @functools.partial(
        pl.kernel, mesh=mesh,
        out_type=jax.ShapeDtypeStruct((n, w), table.dtype),
        scratch_types=[pltpu.VMEM((chunk,), jnp.int32), pltpu.VMEM((chunk, w), table.dtype),
                       pltpu.SemaphoreType.DMA],
    )
    def gather(table_hbm, idx_hbm, out_hbm, idx_v, rows_v, sem):
        base = (lax.axis_index("s") * SC_CORES + lax.axis_index("c")) * per_worker

        @pl.loop(0, per_worker // chunk)
        def _(ci):
            off = base + ci * chunk
            pltpu.sync_copy(idx_hbm.at[pl.ds(off, chunk)], idx_v)
            pltpu.async_copy(table_hbm.at[idx_v], rows_v, sem).wait()
            pltpu.sync_copy(rows_v, out_hbm.at[pl.ds(off, chunk)])

    return gather(table, idx)


def _ffn_kernel(be_ref, nu_ref, xs_ref, w1_ref, b1_ref, w2_ref, b2_ref, ys_ref, w1b_scr, w2b_scr):
    i = pl.program_id(0)
    new_expert = jnp.logical_or(i == 0, be_ref[i] != be_ref[jnp.maximum(i - 1, 0)])

    @pl.when(new_expert)
    def _():
        w1b_scr[...] = w1_ref[0, 0].astype(BF16)
        w2b_scr[...] = w2_ref[0, 0].astype(BF16)

    @pl.when(nu_ref[i] > 0)
    def _():
        live = lax.broadcasted_iota(jnp.int32, xs_ref.shape, 0) < nu_ref[i]
        x = _unpack_pairs(jnp.where(live, xs_ref[...], jnp.uint32(0))).astype(BF16)
        gu = jnp.dot(x, w1b_scr[...], preferred_element_type=F32) + b1_ref[0, 0]
        g = jnp.minimum(gu[:, 0:D], SWIGLU_LIMIT)
        u = jnp.clip(gu[:, D:2 * D], -SWIGLU_LIMIT, SWIGLU_LIMIT)
        act = g * _sigmoid(SWIGLU_ALPHA * g) * (u + 1.0)
        y = jnp.dot(act.astype(BF16), w2b_scr[...], preferred_element_type=F32) + b2_ref[0, 0]
        ys_ref[...] = _pack_pairs(y)

    @pl.when(nu_ref[i] == 0)
    def _():
        ys_ref[...] = jnp.zeros_like(ys_ref)


def _expert_ffn(block_e, n_used, xs, w1, b1, w2, b2, layer):
    n_slots = xs.shape[0]
    bm = MOE_BLOCK
    grid_spec = pltpu.PrefetchScalarGridSpec(
        num_scalar_prefetch=2,
        grid=(n_slots // bm,),
        in_specs=[
            pl.BlockSpec((bm, D // 2), lambda i, be, nu: (i, 0)),
            pl.BlockSpec((1, 1, D, 2 * D), lambda i, be, nu: (layer, be[i], 0, 0)),
            pl.BlockSpec((1, 1, 1, 2 * D), lambda i, be, nu: (layer, be[i], 0, 0)),
            pl.BlockSpec((1, 1, D, D), lambda i, be, nu: (layer, be[i], 0, 0)),
            pl.BlockSpec((1, 1, 1, D), lambda i, be, nu: (layer, be[i], 0, 0)),
        ],
        out_specs=pl.BlockSpec((bm, D // 2), lambda i, be, nu: (i, 0)),
        scratch_shapes=[pltpu.VMEM((D, 2 * D), BF16), pltpu.VMEM((D, D), BF16)],
    )
    return pl.pallas_call(
        _ffn_kernel,
        grid_spec=grid_spec,
        out_shape=jax.ShapeDtypeStruct((n_slots, D // 2), jnp.uint32),
        compiler_params=_cp("arbitrary"),
    )(block_e, n_used, xs, w1, b1, w2, b2)


def _combine_kernel(yg_ref, gate_ref, x_ref, g2_ref, nw_ref, o_ref):
    y = gate_ref[:, 0:1] * _unpack_pairs(yg_ref[0])
    for k in range(1, TOP_K):
        y = y + gate_ref[:, k:k + 1] * _unpack_pairs(yg_ref[k])
    o_ref[...] = x_ref[...] + g2_ref[0] * (_rms(y) * nw_ref[...])


def _combine(yg, gate_t, x1, g2_blocks, nw3, tok_off, tc):
    t = x1.shape[0]
    ob = tok_off // tc
    return pl.pallas_call(
        _combine_kernel,
        grid=(t // tc,),
        in_specs=[pl.BlockSpec((TOP_K, tc, D // 2), lambda i: (0, i + ob, 0)),
                  pl.BlockSpec((tc, TOP_K), lambda i: (i + ob, 0)),
                  pl.BlockSpec((tc, D), lambda i: (i, 0)),
                  pl.BlockSpec((1, 1, D), lambda i: (i, 0, 0)),
                  pl.BlockSpec((1, D), lambda i: (0, 0))],
        out_specs=pl.BlockSpec((tc, D), lambda i: (i, 0)),
        out_shape=jax.ShapeDtypeStruct((t, D), F32),
        compiler_params=_cp("parallel"),
    )(yg, gate_t, x1, g2_blocks, nw3)


def _rope_tables(seqlen):
    t = jnp.arange(seqlen)
    row = (t // GRID_W).astype(F32)
    col = (t % GRID_W).astype(F32)
    inv_freq = ROPE_BASE ** (-jnp.arange(ROPE_PAIRS, dtype=F32) / ROPE_PAIRS)
    ang_r = row[:, None] * inv_freq
    ang_c = col[:, None] * inv_freq
    ang = jnp.concatenate([ang_r, ang_r, ang_c, ang_c], axis=-1)
    ang = jnp.concatenate([ang, ang], axis=-1)
    first = (jnp.arange(128) % 32) < ROPE_PAIRS
    return jnp.cos(ang), jnp.where(first, -jnp.sin(ang), jnp.sin(ang))


def _repack_w_in(w):
    main = jnp.concatenate([w[:, :XBC_W], w[:, XBC_W + DT_W:]], axis=1).astype(BF16)
    wdt = jnp.pad(w[:, XBC_W:XBC_W + DT_W], ((0, 0), (0, 128 - DT_W))).astype(BF16)
    return main, wdt


def _split_dt(dt):
    b, l, _ = dt.shape
    d = dt[:, :, :DT_W].reshape(b, l // CHUNK, CHUNK, 2, SSM_HEADS)
    dc = jnp.moveaxis(d, 3, 0)
    return dc, jnp.swapaxes(dc, -1, -2)


def _moe(lgt, router_b, h2_parts, w1, b1, w2, b2, layer):
    t = lgt.shape[1]
    bm = MOE_BLOCK
    idx, gate, rank, cnt = _route(lgt, router_b.reshape(N_EXPERTS, 1), 256)
    counts = cnt[:, 0].astype(jnp.int32)
    padded = (counts + bm - 1) // bm * bm
    end_padded = jnp.cumsum(padded)
    start_padded = end_padded - padded
    experts = jnp.arange(N_EXPERTS, dtype=jnp.int32)[:, None, None]
    dest = rank + jnp.sum(jnp.where(idx[None] == experts, start_padded[:, None, None], 0), axis=0)
    n_slots = (t * TOP_K + bm - 1) // bm * bm + N_EXPERTS * bm
    n_blocks = n_slots // bm
    n_used = (end_padded[-1] // bm).astype(jnp.int32)
    blk_id = jnp.arange(n_blocks, dtype=jnp.int32)
    blk = jnp.minimum(blk_id, n_used - 1)
    block_e = jnp.sum((end_padded[None, :] <= (blk * bm)[:, None]).astype(jnp.int32), axis=1)
    block_e = jnp.minimum(block_e, N_EXPERTS - 1)
    is_e = block_e[:, None] == jnp.arange(N_EXPERTS, dtype=jnp.int32)[None, :]
    rows_left = jnp.sum(jnp.where(is_e, (start_padded + counts)[None, :], 0), axis=1) - blk_id * bm
    block_rows = jnp.where(blk_id < n_used, jnp.clip(rows_left, 0, bm), 0).astype(jnp.int32)
    h2p = h2_parts[0] if len(h2_parts) == 1 else jnp.concatenate(h2_parts, axis=0)
    xs = _sc_scatter_rows(lax.bitcast_convert_type(h2p, jnp.int32), dest.reshape(-1), n_slots)
    ys = _expert_ffn(block_e, block_rows, lax.bitcast_convert_type(xs, jnp.uint32), w1, b1[:, :, None, :], w2,
                     b2[:, :, None, :], layer)
    return dest, gate.T, ys


def kernel(x, c, ctx, c_ctx, ada_w, ada_b, norm_w, w_in, conv_w, conv_b, dt_bias, a_log, d_skip, ssm_norm_w,
           lambda_qk, subln_w, w_branch, w_out, router_w, router_b, expert_w1, expert_b1, expert_w2, expert_b2):
    bsz, seqlen, _ = x.shape
    ctx_len = ctx.shape[1]
    depth = ada_w.shape[0]
    n_tok = bsz * seqlen
    rope_cos, rope_sin = _rope_tables(seqlen)
    expand = jnp.tile(jnp.repeat(jnp.eye(SSM_HEADS, dtype=BF16), SSM_P, axis=1), (3, 1))
    tm = min(1024, seqlen)
    tq = min(512, seqlen)

    for layer in range(depth):
        last = layer == depth - 1
        lambda_init = 0.8 - 0.6 * math.exp(-0.3 * layer)
        mod = jax.nn.silu(c) @ ada_w[layer] + ada_b[layer]
        mod_c = jax.nn.silu(c_ctx) @ ada_w[layer] + ada_b[layer]
        sh1, sc1, g1, sh2, sc2, g2 = [m[:, None, :] for m in jnp.split(mod, 6, axis=-1)]
        csh1, csc1, cg1, csh2, csc2, cg2 = [jnp.broadcast_to(m[None, None, :], (bsz, 1, D))
                                            for m in jnp.split(mod_c, 6)]
        nw = norm_w[layer]
        w_main, w_dt = _repack_w_in(w_in[layer])

        p, dt = _project(x, nw[0] * (1 + sc1), sh1, w_main, w_dt, tm)
        pc, dt_c = _project(ctx, nw[0] * (1 + csc1), csh1, w_main, w_dt, ctx_len)

        conv_wt = conv_w[layer].T
        conv_bb = conv_b[layer][None, :]
        xconv = _conv_silu(p, pc, conv_wt, conv_bb)
        dtc, dtr = _split_dt(jnp.concatenate([dt, dt_c], axis=1))
        a = -jnp.exp(a_log[layer].astype(F32))
        ys_dir = []
        for d in range(2):
            prow = jnp.stack([dt_bias[layer][d], a[d]], axis=0)
            ys_dir.append(_ssd_scan(xconv, dtc[d], dtr[d], prow, prow.T, expand, seqlen, reverse=(d == 1)))
        y_f, y_b = ys_dir

        lq = lambda_qk[layer].astype(F32)
        lam = (jnp.exp(jnp.sum(lq[0] * lq[1])) - jnp.exp(jnp.sum(lq[2] * lq[3])) + lambda_init).reshape(1)
        wn = (subln_w[layer] * (1.0 - lambda_init))[None, :]
        q_rot, k_rot, v_t = _rope(p, rope_cos, rope_sin, min(512, seqlen))
        kc, vc = COL_K // D, COL_V // D
        v_t_ctx = jnp.swapaxes(pc[:, :, COL_V:COL_V + D], 1, 2)
        y_attn = _attention_latent(lam, q_rot, k_rot, v_t, pc, kc, v_t_ctx, wn, tq, min(512, seqlen // 2))

        zr, zi = _fourier_channel(p, min(512, seqlen))
        y_four = _fourier_seq(zr, zi)

        dsk = jnp.repeat(d_skip[layer], SSM_P)[None, :]
        nsw = ssm_norm_w[layer][None, :]
        wb = w_branch[layer].astype(BF16)
        wo = w_out[layer].astype(BF16)
        rwt = router_w[layer].T
        tmm = min(256, seqlen)
        x1, h2, lgt = _merge(y_f, y_b, xconv, 0, p, y_attn, y_four, x, dsk, nsw, wb, wo, nw[1][None, :],
                             g1, nw[2] * (1 + sc2), sh2, rwt, tmm)
        h2_parts = [h2.reshape(n_tok, D // 2)]
        lgt = jnp.moveaxis(lgt, 0, 1).reshape(N_EXPERTS, n_tok)

        if not last:
            y_attn_c = _attention(lam, pc, COL_Q // D, [(pc, kc, pc, vc)], wn, ctx_len, ctx_len)
            zr_c, zi_c = _fourier_channel(pc, ctx_len)
            y_four_c = _fourier_seq_direct(zr_c, zi_c)
            ctx1, hc2, lgt_c = _merge(y_f, y_b, xconv, seqlen // ctx_len, pc, y_attn_c, y_four_c, ctx, dsk, nsw,
                                      wb, wo, nw[1][None, :], cg1, nw[2] * (1 + csc2), csh2, rwt, ctx_len)
            h2_parts.append(hc2.reshape(bsz * ctx_len, D // 2))
            lgt = jnp.concatenate([lgt, jnp.moveaxis(lgt_c, 0, 1).reshape(N_EXPERTS, bsz * ctx_len)], axis=1)

        dest, gate_t, ys = _moe(lgt, router_b[layer], h2_parts, expert_w1, expert_b1, expert_w2, expert_b2, layer)
        n_routed = dest.shape[1]
        yg = _sc_gather_rows(lax.bitcast_convert_type(ys, jnp.int32), dest.reshape(-1))
        yg = lax.bitcast_convert_type(yg, jnp.uint32).reshape(TOP_K, n_routed, D // 2)
        tc = 256
        g2_blocks = jnp.repeat(g2, seqlen // tc, axis=0)
        x = _combine(yg, gate_t, x1.reshape(n_tok, D), g2_blocks, nw[3][None, :], 0, tc).reshape(x.shape)
        if not last:
            cg2_blocks = jnp.repeat(cg2, ctx_len // tc, axis=0)
            ctx = _combine(yg, gate_t, ctx1.reshape(bsz * ctx_len, D), cg2_blocks, nw[3][None, :],
                           n_tok, tc).reshape(ctx.shape)
    return x
```

```python
import functools
import math

import numpy as np
import jax
import jax.numpy as jnp
from jax import lax
from jax.experimental import pallas as pl
from jax.experimental.pallas import tpu as pltpu
from jax.experimental.pallas import tpu_sc as plsc

F32 = jnp.float32
BF16 = jnp.bfloat16
HI = lax.Precision.HIGHEST

D = 1024
GRID_W = 64
EPS = 1e-6
SSM_HEADS = 16
SSM_P = 64
N_GROUPS = 4
D_STATE = 128
CHUNK = 128
XBC_W = D + 2 * N_GROUPS * D_STATE
DT_W = 2 * SSM_HEADS
HEAD_DIM = 64
ATTN_HEADS = 8
ATTN_SCALE = HEAD_DIM ** -0.5
ROPE_BASE = 10000.0
ROPE_PAIRS = 16
FOURIER_GROUPS = 4
FOURIER_GW = 256
N_EXPERTS = 32
TOP_K = 4
SWIGLU_LIMIT = 7.0
SWIGLU_ALPHA = 1.702
LOG2E = 1.4426950408889634

COL_XBC, COL_K, COL_V, COL_Z, COL_Q, COL_F, COL_G = 0, 2048, 3072, 4096, 5120, 6144, 7168
NP = 10240

VMEM_LIMIT = 52 * 1024 * 1024
MOE_BLOCK = 512


def _cp(*sem):
    return pltpu.CompilerParams(dimension_semantics=sem, vmem_limit_bytes=VMEM_LIMIT)


def _sigmoid(x):
    return 0.5 * jnp.tanh(0.5 * x) + 0.5


def _rms(x):
    return x * lax.rsqrt(jnp.mean(x * x, axis=-1, keepdims=True) + EPS)


def _pack_pairs(x):
    half = x.shape[1] // 2
    lo = lax.bitcast_convert_type(x[:, :half].astype(BF16).astype(F32), jnp.uint32)
    hi = lax.bitcast_convert_type(x[:, half:].astype(BF16).astype(F32), jnp.uint32)
    return lax.bitcast_convert_type((lo >> 16) | (hi & jnp.uint32(0xFFFF0000)), jnp.int32)


def _unpack_pairs(w):
    u = lax.bitcast_convert_type(w, jnp.uint32)
    lo = lax.bitcast_convert_type(u << 16, F32)
    hi = lax.bitcast_convert_type(u & jnp.uint32(0xFFFF0000), F32)
    return jnp.concatenate([lo, hi], axis=1)


def _proj_kernel(x_ref, mul_ref, add_ref, w_ref, wdt_ref, o_ref, dt_ref, h_scr):
    @pl.when(pl.program_id(2) == 0)
    def _():
        h = _rms(x_ref[0]) * mul_ref[0] + add_ref[0]
        hb = h.astype(BF16)
        h_scr[...] = hb
        dt_ref[0] = jnp.dot(hb, wdt_ref[...], preferred_element_type=F32)

    o_ref[0] = jnp.dot(h_scr[...], w_ref[...], preferred_element_type=F32).astype(BF16)


def _project(x, mul, add, w, wdt, tm):
    b, l, _ = x.shape
    tn = 1024
    return pl.pallas_call(
        _proj_kernel,
        grid=(b, l // tm, NP // tn),
        in_specs=[
            pl.BlockSpec((1, tm, D), lambda bi, i, j: (bi, i, 0)),
            pl.BlockSpec((1, 1, D), lambda bi, i, j: (bi, 0, 0)),
            pl.BlockSpec((1, 1, D), lambda bi, i, j: (bi, 0, 0)),
            pl.BlockSpec((D, tn), lambda bi, i, j: (0, j)),
            pl.BlockSpec((D, 128), lambda bi, i, j: (0, 0)),
        ],
        out_specs=[
            pl.BlockSpec((1, tm, tn), lambda bi, i, j: (bi, i, j)),
            pl.BlockSpec((1, tm, 128), lambda bi, i, j: (bi, i, 0)),
        ],
        out_shape=[jax.ShapeDtypeStruct((b, l, NP), BF16), jax.ShapeDtypeStruct((b, l, 128), F32)],
        scratch_shapes=[pltpu.VMEM((tm, D), BF16)],
        compiler_params=_cp("parallel", "parallel", "arbitrary"),
    )(x, mul, add, w, wdt)


def _conv_kernel(x_ref, prev_ref, next_ref, xc_ref, w_ref, b_ref, o_ref, ext_scr):
    i = pl.program_id(1)
    nl = pl.num_programs(1) - 1
    tr = x_ref.shape[1]
    is_ctx = i == nl
    prev = prev_ref[0].astype(F32)[8:16]
    nxt = next_ref[0].astype(F32)[0:8]
    ext_scr[0:8, :] = jnp.where(jnp.logical_and(i > 0, i < nl), prev, 0.0)
    ext_scr[8:8 + tr, :] = jnp.where(is_ctx, xc_ref[0], x_ref[0]).astype(F32)
    ext_scr[8 + tr:16 + tr, :] = jnp.where(i < nl - 1, nxt, 0.0)
    acc = b_ref[...] + w_ref[0:1, :] * ext_scr[6:6 + tr, :]
    for k in range(1, 5):
        acc = acc + w_ref[k:k + 1, :] * ext_scr[6 + k:6 + k + tr, :]
    o_ref[0] = (acc * _sigmoid(acc)).astype(BF16)


def _conv_silu(p, pc, conv_wt, conv_b):
    b, l, _ = p.shape
    tr = pc.shape[1]
    assert l % tr == 0 and tr % 16 == 0
    nl = l // tr
    nh = tr // 16
    last = l // 16 - 1
    return pl.pallas_call(
        _conv_kernel,
        grid=(b, nl + 1),
        in_specs=[
            pl.BlockSpec((1, tr, XBC_W), lambda bi, i: (bi, jnp.minimum(i, nl - 1), 0)),
            pl.BlockSpec((1, 16, XBC_W), lambda bi, i: (bi, jnp.clip(i * nh - 1, 0, last), 0)),
            pl.BlockSpec((1, 16, XBC_W), lambda bi, i: (bi, jnp.minimum((i + 1) * nh, last), 0)),
            pl.BlockSpec((1, tr, XBC_W), lambda bi, i: (bi, 0, 0)),
            pl.BlockSpec((5, XBC_W), lambda bi, i: (0, 0)),
            pl.BlockSpec((1, XBC_W), lambda bi, i: (0, 0)),
        ],
        out_specs=pl.BlockSpec((1, tr, XBC_W), lambda bi, i: (bi, i, 0)),
        out_shape=jax.ShapeDtypeStruct((b, l + tr, XBC_W), BF16),
        scratch_shapes=[pltpu.VMEM((tr + 16, XBC_W), F32)],
        compiler_params=_cp("parallel", "parallel"),
    )(p, p, p, pc, conv_wt, conv_b)


def _softplus(x):
    return jnp.maximum(x, 0.0) + jnp.log1p(jnp.exp(-jnp.abs(x)))


def _split3(v):
    hi = v.astype(BF16)
    r = v - hi.astype(F32)
    mid = r.astype(BF16)
    lo = (r - mid.astype(F32)).astype(BF16)
    return hi, mid, lo


def _ssd_kernel(xbc_ref, dtc_ref, dtr_ref, prow_ref, pcol_ref, e_ref, o_ref, state_scr, *, reverse):
    @pl.when(pl.program_id(1) == 0)
    def _():
        state_scr[...] = jnp.zeros_like(state_scr)

    q = CHUNK
    row = lax.broadcasted_iota(jnp.int32, (q, q), 0)
    col = lax.broadcasted_iota(jnp.int32, (q, q), 1)
    keep = (col >= row) if reverse else (col <= row)
    tri = keep.astype(BF16)
    tri_t = ((row >= col) if reverse else (row <= col)).astype(BF16)
    nh = SSM_HEADS

    dt_col = _softplus(dtc_ref[0, 0] + prow_ref[0:1, :])
    dt_row = _softplus(dtr_ref[0, 0] + pcol_ref[:, 0:1])
    adt_col = dt_col * prow_ref[1:2, :]
    adt_row = dt_row * pcol_ref[:, 1:2]
    c3 = jnp.dot(tri, jnp.concatenate(_split3(adt_col), axis=1), preferred_element_type=F32)
    acs_col = c3[:, 0:nh] + c3[:, nh:2 * nh] + c3[:, 2 * nh:3 * nh]
    r3 = jnp.dot(jnp.concatenate(_split3(adt_row), axis=0), tri_t, preferred_element_type=F32)
    acs_row = r3[0:nh] + r3[nh:2 * nh] + r3[2 * nh:3 * nh]
    a_tot = acs_col[0:1, :] if reverse else acs_col[q - 1:q, :]

    expand3 = e_ref[...]

    def expand(v):
        return jnp.dot(jnp.concatenate(_split3(v), axis=1), expand3, preferred_element_type=F32)

    wend = expand(jnp.exp(a_tot - acs_col) * dt_col)
    dec_in = expand(jnp.exp(acs_col))
    dec_state = dec_in[0:1, :] if reverse else dec_in[q - 1:q, :]

    xb = xbc_ref[0, :, 0:D]
    xw = (xb.astype(F32) * wend).astype(BF16)
    lane = lax.broadcasted_iota(jnp.int32, (q, 2 * SSM_P), 1)
    gw = (SSM_HEADS // N_GROUPS) * SSM_P
    for g in range(N_GROUPS):
        bg = xbc_ref[0, :, D + g * D_STATE:D + (g + 1) * D_STATE]
        cg = xbc_ref[0, :, D + N_GROUPS * D_STATE + g * D_STATE:D + N_GROUPS * D_STATE + (g + 1) * D_STATE]
        st = state_scr[:, g * gw:(g + 1) * gw]
        y_off = jnp.dot(cg, st.astype(BF16), preferred_element_type=F32)
        s_new = lax.dot_general(bg, xw[:, g * gw:(g + 1) * gw], (((0,), (0,)), ((), ())),
                                preferred_element_type=F32)
        state_scr[:, g * gw:(g + 1) * gw] = st * dec_state[:, g * gw:(g + 1) * gw] + s_new
        cb = lax.dot_general(cg, bg, (((1,), (1,)), ((), ())), preferred_element_type=F32)
        for pair in range(2):
            h0 = 4 * g + 2 * pair
            ws = []
            for h in (h0, h0 + 1):
                seg = acs_col[:, h:h + 1] - acs_row[h:h + 1, :]
                ws.append(jnp.exp(jnp.where(keep, seg, -jnp.inf)) * cb * dt_row[h:h + 1, :])
            lhs = jnp.concatenate(ws, axis=1).astype(BF16)
            xp = xb[:, h0 * SSM_P:(h0 + 2) * SSM_P]
            zero = jnp.zeros_like(xp)
            rhs = jnp.concatenate([jnp.where(lane < SSM_P, xp, zero), jnp.where(lane >= SSM_P, xp, zero)], axis=0)
            y_diag = jnp.dot(lhs, rhs, preferred_element_type=F32)
            c0 = h0 * SSM_P
            y = y_off[:, pair * 128:(pair + 1) * 128] * dec_in[:, c0:c0 + 128] + y_diag
            o_ref[0, :, c0:c0 + 128] = y.astype(BF16)


def _ssd_scan(xbc, dtc, dtr, prow, pcol, expand, n_lat, reverse):
    b, lt, _ = xbc.shape
    nt = lt // CHUNK
    nl = n_lat // CHUNK
    nc = nt - nl
    if reverse:
        chunk = lambda s: nt - 1 - s
    else:
        chunk = lambda s: jnp.where(s < nc, nl + s, s - nc)
    return pl.pallas_call(
        functools.partial(_ssd_kernel, reverse=reverse),
        grid=(b, nt),
        in_specs=[
            pl.BlockSpec((1, CHUNK, XBC_W), lambda bi, s: (bi, chunk(s), 0)),
            pl.BlockSpec((1, 1, CHUNK, SSM_HEADS), lambda bi, s: (bi, chunk(s), 0, 0)),
            pl.BlockSpec((1, 1, SSM_HEADS, CHUNK), lambda bi, s: (bi, chunk(s), 0, 0)),
            pl.BlockSpec((2, SSM_HEADS), lambda bi, s: (0, 0)),
            pl.BlockSpec((SSM_HEADS, 2), lambda bi, s: (0, 0)),
            pl.BlockSpec((3 * SSM_HEADS, D), lambda bi, s: (0, 0)),
        ],
        out_specs=pl.BlockSpec((1, CHUNK, D), lambda bi, s: (bi, chunk(s), 0)),
        out_shape=jax.ShapeDtypeStruct((b, lt, D), BF16),
        scratch_shapes=[pltpu.VMEM((D_STATE, D), F32)],
        compiler_params=_cp("parallel", "arbitrary"),
    )(xbc, dtc, dtr, prow, pcol, expand)


def _rope_kernel(q_ref, k_ref, v_ref, cos_ref, sin_ref, qo_ref, ko_ref, vt_ref):
    vt_ref[0] = v_ref[0].astype(F32).T.astype(BF16)
    cos = jnp.tile(cos_ref[...], (1, D // 128))
    sin = jnp.tile(sin_ref[...], (1, D // 128))
    lane = lax.broadcasted_iota(jnp.int32, cos.shape, 1)
    first = (lane % 32) < ROPE_PAIRS

    def rot(t):
        nxt = pltpu.roll(t, D - ROPE_PAIRS, axis=1)
        prv = pltpu.roll(t, ROPE_PAIRS, axis=1)
        return t * cos + jnp.where(first, nxt, prv) * sin

    qo_ref[0] = rot(q_ref[0].astype(F32)).astype(BF16)
    ko_ref[0] = rot(k_ref[0].astype(F32)).astype(BF16)


def _rope(p, cos, sin_signed, tr):
    b, l, _ = p.shape
    spec = lambda c: pl.BlockSpec((1, tr, D), lambda bi, i: (bi, i, c))
    tab = pl.BlockSpec((tr, 128), lambda bi, i: (i, 0))
    out = pl.BlockSpec((1, tr, D), lambda bi, i: (bi, i, 0))
    return pl.pallas_call(
        _rope_kernel,
        grid=(b, l // tr),
        in_specs=[spec(COL_Q // D), spec(COL_K // D), spec(COL_V // D), tab, tab],
        out_specs=[out, out, pl.BlockSpec((1, D, tr), lambda bi, i: (bi, 0, i))],
        out_shape=[jax.ShapeDtypeStruct((b, l, D), BF16)] * 2 + [jax.ShapeDtypeStruct((b, D, l), BF16)],
        compiler_params=_cp("parallel", "parallel"),
    )(p, p, p, cos, sin_signed)


ONES_ROWS = 16


def _attn_lat_kernel(lam_ref, q_ref, k_ref, vt_ref, kc_ref, vtc_ref, wn_ref, o_ref,
                     qst_scr, m_scr, acc_scr, s_scr, *, tq, tk, scale):
    q = q_ref[0].astype(F32) * scale
    lane = lax.broadcasted_iota(jnp.int32, q.shape, 1)
    qs = jnp.concatenate([jnp.where(lane < HEAD_DIM, q, 0.0), jnp.where(lane >= HEAD_DIM, q, 0.0)], axis=0)
    qst_scr[...] = qs.T.astype(BF16)
    m_scr[...] = jnp.full_like(m_scr, -1e30)
    acc_scr[...] = jnp.zeros_like(acc_scr)
    n = k_ref.shape[1] // tk

    def qk(j, slot):
        off = pl.multiple_of(j * tk, tk)
        s_scr[slot] = jnp.dot(k_ref[0, pl.ds(off, tk), :], qst_scr[...], preferred_element_type=F32)

    def soft_pv(st, vt_blk):
        vt_ext = jnp.concatenate([vt_blk, jnp.ones((ONES_ROWS, vt_blk.shape[1]), BF16)], axis=0)
        m_prev = m_scr[...]
        m_new = jnp.maximum(m_prev, jnp.max(st, axis=0, keepdims=True))
        alpha = jnp.exp2(m_prev - m_new)
        pt = jnp.exp2((st - m_new).astype(BF16))
        acc_scr[...] = alpha * acc_scr[...] + jnp.dot(vt_ext, pt, preferred_element_type=F32)
        m_scr[...] = m_new

    def use(j, slot):
        off = pl.multiple_of(j * tk, tk)
        soft_pv(s_scr[slot], vt_ref[0, :, pl.ds(off, tk)])

    qk(0, 0)

    def body(i, carry):
        j = 2 * i
        qk(j + 1, 1)
        use(j, 0)
        qk(jnp.minimum(j + 2, n - 1), 0)
        use(j + 1, 1)
        return carry

    lax.fori_loop(0, n // 2, body, 0)
    soft_pv(jnp.dot(kc_ref[0], qst_scr[...], preferred_element_type=F32), vtc_ref[0])

    acc = acc_scr[...]
    o1 = acc[0:128, 0:tq] / acc[128:129, 0:tq]
    o2 = acc[0:128, tq:2 * tq] / acc[128:129, tq:2 * tq]
    o = (o1 - lam_ref[0] * o2).T
    o_ref[0] = (_rms(o) * wn_ref[...]).astype(BF16)


def _attention_latent(lam, q, k, vt, kc, kc_col, vtc, wn, tq, tk):
    b, sq = q.shape[0], q.shape[1]
    assert (k.shape[1] // tk) % 2 == 0
    return pl.pallas_call(
        functools.partial(_attn_lat_kernel, tq=tq, tk=tk, scale=ATTN_SCALE * LOG2E),
        grid=(b, ATTN_HEADS, sq // tq),
        in_specs=[pl.BlockSpec(memory_space=pltpu.SMEM),
                  pl.BlockSpec((1, tq, 128), lambda bi, h, i: (bi, i, h)),
                  pl.BlockSpec((1, k.shape[1], 128), lambda bi, h, i: (bi, 0, h)),
                  pl.BlockSpec((1, 128, vt.shape[2]), lambda bi, h, i: (bi, h, 0)),
                  pl.BlockSpec((1, kc.shape[1], 128), lambda bi, h, i: (bi, 0, kc_col * ATTN_HEADS + h)),
                  pl.BlockSpec((1, 128, vtc.shape[2]), lambda bi, h, i: (bi, h, 0)),
                  pl.BlockSpec((1, 128), lambda bi, h, i: (0, 0))],
        out_specs=pl.BlockSpec((1, tq, 128), lambda bi, h, i: (bi, i, h)),
        out_shape=jax.ShapeDtypeStruct((b, sq, D), BF16),
        scratch_shapes=[pltpu.VMEM((128, 2 * tq), BF16), pltpu.VMEM((1, 2 * tq), F32),
                        pltpu.VMEM((128 + ONES_ROWS, 2 * tq), F32), pltpu.VMEM((2, tk, 2 * tq), F32)],
        compiler_params=_cp("parallel", "parallel", "parallel"),
    )(lam, q, k, vt, kc, vtc, wn)


def _attn_kernel(lam_ref, q_ref, *refs, n_src, tq, tk, scale):
    kv = refs[:2 * n_src]
    wn_ref, o_ref, qs_scr, m_scr, acc_scr = refs[2 * n_src:]
    q = q_ref[0].astype(F32) * scale
    lane = lax.broadcasted_iota(jnp.int32, q.shape, 1)
    qs_scr[0:tq, :] = jnp.where(lane < HEAD_DIM, q, 0.0).astype(BF16)
    qs_scr[tq:2 * tq, :] = jnp.where(lane >= HEAD_DIM, q, 0.0).astype(BF16)
    m_scr[...] = jnp.full_like(m_scr, -1e30)
    acc_scr[...] = jnp.zeros_like(acc_scr)

    def chunk(k_blk, v_blk):
        qs = qs_scr[...]
        s = lax.dot_general(qs, k_blk, (((1,), (1,)), ((), ())), preferred_element_type=F32)
        m_prev = m_scr[...]
        m_new = jnp.maximum(m_prev, jnp.max(s, axis=-1, keepdims=True))
        alpha = jnp.exp2(m_prev - m_new)
        p = jnp.exp2((s - m_new).astype(BF16))
        v_ext = jnp.concatenate([v_blk, jnp.ones_like(v_blk)], axis=1)
        acc_scr[...] = alpha * acc_scr[...] + jnp.dot(p, v_ext, preferred_element_type=F32)
        m_scr[...] = m_new

    for si in range(n_src):
        k_ref, v_ref = kv[2 * si], kv[2 * si + 1]
        n_keys = k_ref.shape[1]
        step = min(tk, n_keys)

        def body(j, carry, k_ref=k_ref, v_ref=v_ref, step=step):
            off = pl.multiple_of(j * step, step)
            chunk(k_ref[0, pl.ds(off, step), :], v_ref[0, pl.ds(off, step), :])
            return carry

        lax.fori_loop(0, n_keys // step, body, 0)

    acc = acc_scr[...]
    o1 = acc[0:tq, 0:128] / acc[0:tq, 128:256]
    o2 = acc[tq:2 * tq, 0:128] / acc[tq:2 * tq, 128:256]
    o = o1 - lam_ref[0] * o2
    o_ref[0] = (_rms(o) * wn_ref[...]).astype(BF16)


def _attention(lam, q, q_col, kvs, wn, tq, tk):
    b, sq = q.shape[0], q.shape[1]
    in_specs = [pl.BlockSpec(memory_space=pltpu.SMEM),
                pl.BlockSpec((1, tq, 128), lambda bi, h, i, c=q_col: (bi, i, c * ATTN_HEADS + h))]
    args = [lam, q]
    for k, kc, v, vc in kvs:
        in_specs.append(pl.BlockSpec((1, k.shape[1], 128), lambda bi, h, i, c=kc: (bi, 0, c * ATTN_HEADS + h)))
        in_specs.append(pl.BlockSpec((1, v.shape[1], 128), lambda bi, h, i, c=vc: (bi, 0, c * ATTN_HEADS + h)))
        args += [k, v]
    in_specs.append(pl.BlockSpec((1, 128), lambda bi, h, i: (0, 0)))
    args.append(wn)
    return pl.pallas_call(
        functools.partial(_attn_kernel, n_src=len(kvs), tq=tq, tk=tk, scale=ATTN_SCALE * LOG2E),
        grid=(b, ATTN_HEADS, sq // tq),
        in_specs=in_specs,
        out_specs=pl.BlockSpec((1, tq, 128), lambda bi, h, i: (bi, i, h)),
        out_shape=jax.ShapeDtypeStruct((b, sq, D), BF16),
        scratch_shapes=[pltpu.VMEM((2 * tq, 128), BF16), pltpu.VMEM((2 * tq, 1), F32),
                        pltpu.VMEM((2 * tq, 256), F32)],
        compiler_params=_cp("parallel", "parallel", "parallel"),
    )(*args)


def _dft_tables(n):
    k = np.arange(n)
    ang = 2.0 * np.pi * ((k[:, None] * k[None, :]) % n) / n
    return np.cos(ang), np.sin(ang)


def _four0_kernel(x_ref, w_ref, zr_ref, zi_ref):
    y = jnp.dot(x_ref[0], w_ref[...], preferred_element_type=F32)
    zr_ref[0] = y[:, 0:FOURIER_GW].astype(BF16)
    zi_ref[0] = y[:, FOURIER_GW:2 * FOURIER_GW].astype(BF16)


def _fourier_channel(p, tm):
    b, l, _ = p.shape
    c, s = _dft_tables(FOURIER_GW)
    w0 = jnp.asarray(np.concatenate([c, s], axis=1) / math.sqrt(FOURIER_GW), BF16)
    cb = COL_F // FOURIER_GW
    out = pl.BlockSpec((1, tm, FOURIER_GW), lambda bi, i, g: (bi, i, g))
    return pl.pallas_call(
        _four0_kernel,
        grid=(b, l // tm, FOURIER_GROUPS),
        in_specs=[pl.BlockSpec((1, tm, FOURIER_GW), lambda bi, i, g: (bi, i, cb + g)),
                  pl.BlockSpec((FOURIER_GW, 2 * FOURIER_GW), lambda bi, i, g: (0, 0))],
        out_specs=[out, out],
        out_shape=[jax.ShapeDtypeStruct((b, l, D), BF16)] * 2,
        compiler_params=_cp("parallel", "parallel", "parallel"),
    )(p, w0)


SEQ_INNER = 128
SEQ_GROUP = 16


LANES = 128


def _to_lane_tiles(scr, x):
    for c in range(scr.shape[0]):
        scr[c] = x[:, c * LANES:(c + 1) * LANES]


def _from_lane_tiles(scr):
    return jnp.concatenate([scr[c] for c in range(scr.shape[0])], axis=1)


def _strided_rows(scr, rows):
    return jnp.concatenate([scr[c, rows, :] for c in range(scr.shape[0])], axis=1)


def _store_strided_rows(scr, rows, x):
    for c in range(scr.shape[0]):
        scr[c, rows, :] = x[:, c * LANES:(c + 1) * LANES]


def _four1_kernel(zr_ref, zi_ref, m_ref, c_ref, xr_scr, xi_scr, or_scr, oi_scr):
    na = zr_ref.shape[1]
    g = SEQ_GROUP
    _to_lane_tiles(xr_scr, zr_ref[0].astype(F32).reshape(na * g, D))
    _to_lane_tiles(xi_scr, zi_ref[0].astype(F32).reshape(na * g, D))
    for j in range(g):
        rows = pl.ds(j, na, stride=g)
        x = jnp.concatenate([_strided_rows(xr_scr, rows), _strided_rows(xi_scr, rows)], axis=0)
        res = jnp.dot(m_ref[j], x.astype(BF16), preferred_element_type=F32)
        _store_strided_rows(or_scr, rows, res[0:na])
        _store_strided_rows(oi_scr, rows, res[na:2 * na])
    c_ref[0, 0] = _from_lane_tiles(or_scr).reshape(na, g, D).astype(BF16)
    c_ref[0, 1] = _from_lane_tiles(oi_scr).reshape(na, g, D).astype(BF16)


def _four2_kernel(c_ref, w_ref, o_ref, o_scr):
    g = c_ref.shape[2]
    nb = SEQ_INNER
    for j in range(g):
        x = jnp.concatenate([c_ref[0, 0, j], c_ref[0, 1, j]], axis=0)
        _store_strided_rows(o_scr, pl.ds(j, nb, stride=g), jnp.dot(w_ref[...], x, preferred_element_type=F32))
    o_ref[0] = _from_lane_tiles(o_scr).reshape(nb, g, D).astype(BF16)


def _fourier_seq(zr, zi):
    b, l, _ = zr.shape
    nb, g = SEQ_INNER, SEQ_GROUP
    na = l // nb
    g2 = min(g, na)
    assert l % nb == 0 and na % g2 == 0
    ka = np.arange(na)
    ang_a = 2.0 * np.pi * ((ka[:, None] * ka[None, :]) % na) / na
    tw = 2.0 * np.pi * ((np.arange(nb)[:, None] * ka[None, :]) % l) / l
    ang = ang_a[None, :, :] + tw[:, :, None]
    cr, si = np.cos(ang) / math.sqrt(na), np.sin(ang) / math.sqrt(na)
    m = jnp.asarray(np.concatenate([np.concatenate([cr, -si], axis=2),
                                    np.concatenate([si, cr], axis=2)], axis=1), BF16)
    c2, s2 = _dft_tables(nb)
    w2 = jnp.asarray(np.concatenate([c2, -s2], axis=1) / math.sqrt(nb), BF16)

    zin = pl.BlockSpec((1, na, g, D), lambda bi, j: (bi, 0, j, 0))
    c = pl.pallas_call(
        _four1_kernel,
        grid=(b, nb // g),
        in_specs=[zin, zin, pl.BlockSpec((g, 2 * na, 2 * na), lambda bi, j: (j, 0, 0))],
        out_specs=pl.BlockSpec((1, 2, na, g, D), lambda bi, j: (bi, 0, 0, j, 0)),
        out_shape=jax.ShapeDtypeStruct((b, 2, na, nb, D), BF16),
        scratch_shapes=[pltpu.VMEM((D // LANES, na * g, LANES), F32)] * 4,
        compiler_params=_cp("parallel", "parallel"),
    )(zr.reshape(b, na, nb, D), zi.reshape(b, na, nb, D), m)
    r = pl.pallas_call(
        _four2_kernel,
        grid=(b, na // g2),
        in_specs=[pl.BlockSpec((1, 2, g2, nb, D), lambda bi, j: (bi, 0, j, 0, 0)),
                  pl.BlockSpec((nb, 2 * nb), lambda bi, j: (0, 0))],
        out_specs=pl.BlockSpec((1, nb, g2, D), lambda bi, j: (bi, 0, j, 0)),
        out_shape=jax.ShapeDtypeStruct((b, nb, na, D), BF16),
        scratch_shapes=[pltpu.VMEM((D // LANES, nb * g2, LANES), F32)],
        compiler_params=_cp("parallel", "parallel"),
    )(c, w2)
    return r.reshape(b, l, D)


def _four_direct_kernel(zr_ref, zi_ref, w_ref, o_ref):
    z = jnp.concatenate([zr_ref[0], zi_ref[0]], axis=0)
    o_ref[0] = jnp.dot(w_ref[...], z, preferred_element_type=F32).astype(BF16)


def _fourier_seq_direct(zr, zi):
    b, l, _ = zr.shape
    c, s = _dft_tables(l)
    w = jnp.asarray(np.concatenate([c, -s], axis=1) / math.sqrt(l), BF16)
    blk = pl.BlockSpec((1, l, D), lambda bi: (bi, 0, 0))
    return pl.pallas_call(
        _four_direct_kernel,
        grid=(b,),
        in_specs=[blk, blk, pl.BlockSpec((l, 2 * l), lambda bi: (0, 0))],
        out_specs=blk,
        out_shape=jax.ShapeDtypeStruct((b, l, D), BF16),
        compiler_params=_cp("parallel"),
    )(zr, zi, w)


def _merge_kernel(yf_ref, yb_ref, xs_ref, z_ref, at_ref, fo_ref, g0_ref, g1_ref, g2_ref, x_ref,
                  dsk_ref, nsw_ref, wb_ref, wo_ref, nw1_ref, gate_ref, mul2_ref, add2_ref, rwt_ref,
                  x1_ref, h2_ref, lg_ref):
    xs = xs_ref[0].astype(F32)
    z = z_ref[0].astype(F32)
    y = (yf_ref[0].astype(F32) + yb_ref[0].astype(F32) + dsk_ref[...] * xs) * (z * _sigmoid(z))
    y = (_rms(y) * nsw_ref[...]).astype(BF16)
    m = _sigmoid(g0_ref[0].astype(F32)) * jnp.dot(y, wb_ref[0], preferred_element_type=F32)
    m = m + _sigmoid(g1_ref[0].astype(F32)) * jnp.dot(at_ref[0], wb_ref[1], preferred_element_type=F32)
    m = m + _sigmoid(g2_ref[0].astype(F32)) * jnp.dot(fo_ref[0], wb_ref[2], preferred_element_type=F32)
    out = jnp.dot(m.astype(BF16), wo_ref[...], preferred_element_type=F32)
    x1 = x_ref[0] + gate_ref[0] * (_rms(out) * nw1_ref[...])
    h2 = _rms(x1) * mul2_ref[0] + add2_ref[0]
    x1_ref[0] = x1
    h2_ref[0] = _pack_pairs(h2)
    lg_ref[0] = lax.dot_general(rwt_ref[...], h2, (((1,), (1,)), ((), ())), precision=HI,
                                preferred_element_type=F32)


def _merge(yf, yb, xconv, row_off, p, attn, four, x, dsk, nsw, wb, wo, nw1, gate1, mul2, add2, rwt, tm):
    b, l, _ = x.shape
    nb = l // tm
    row = lambda c=0: pl.BlockSpec((1, tm, D), lambda bi, i, c=c: (bi, i, c))
    off = lambda: pl.BlockSpec((1, tm, D), lambda bi, i: (bi, i + row_off, 0))
    vec = pl.BlockSpec((1, D), lambda bi, i: (0, 0))
    bvec = pl.BlockSpec((1, 1, D), lambda bi, i: (bi, 0, 0))
    return pl.pallas_call(
        _merge_kernel,
        grid=(b, nb),
        in_specs=[off(), off(), off(), row(COL_Z // D), row(), row(),
                  row(COL_G // D), row(COL_G // D + 1), row(COL_G // D + 2), row(),
                  vec, vec, pl.BlockSpec((3, D, D), lambda bi, i: (0, 0, 0)),
                  pl.BlockSpec((D, D), lambda bi, i: (0, 0)), vec, bvec, bvec, bvec,
                  pl.BlockSpec((N_EXPERTS, D), lambda bi, i: (0, 0))],
        out_specs=[row(), pl.BlockSpec((1, tm, D // 2), lambda bi, i: (bi, i, 0)),
                   pl.BlockSpec((1, N_EXPERTS, tm), lambda bi, i: (bi, 0, i))],
        out_shape=[jax.ShapeDtypeStruct((b, l, D), F32), jax.ShapeDtypeStruct((b, l, D // 2), jnp.int32),
                   jax.ShapeDtypeStruct((b, N_EXPERTS, l), F32)],
        compiler_params=_cp("parallel", "parallel"),
    )(yf, yb, xconv, p, attn, four, p, p, p, x, dsk, nsw, wb, wo, nw1, gate1, mul2, add2, rwt)


def _router_kernel(lg_ref, bias_ref, idx_ref, gate_ref, rank_ref, cnt_ref, base_scr):
    @pl.when(pl.program_id(0) == 0)
    def _():
        base_scr[...] = jnp.zeros_like(base_scr)

    lg = lg_ref[...] + bias_ref[...]
    tr = lg.shape[1]
    eid = lax.broadcasted_iota(jnp.int32, lg.shape, 0)
    work = lg
    vals, hots = [], []
    for k in range(TOP_K):
        mx = jnp.max(work, axis=0, keepdims=True)
        sel = jnp.min(jnp.where(work == mx, eid, N_EXPERTS), axis=0, keepdims=True)
        hot = eid == sel
        idx_ref[k:k + 1, :] = sel
        vals.append(mx)
        hots.append(hot)
        work = jnp.where(hot, -jnp.inf, work)
    es = [jnp.exp(v - vals[0]) for v in vals]
    den = es[0] + es[1] + es[2] + es[3]
    for k in range(TOP_K):
        gate_ref[k:k + 1, :] = es[k] / den

    multi = jnp.zeros(lg.shape, F32)
    for hot in hots:
        multi = multi + hot.astype(F32)
    r = lax.broadcasted_iota(jnp.int32, (tr, tr), 0)
    c = lax.broadcasted_iota(jnp.int32, (tr, tr), 1)
    before = (r < c).astype(BF16)
    cnt = jnp.dot(multi.astype(BF16), before, preferred_element_type=F32) + base_scr[:, 0:1]
    for k in range(TOP_K):
        rank = jnp.sum(jnp.where(hots[k], cnt, 0.0), axis=0, keepdims=True)
        rank_ref[k:k + 1, :] = rank.astype(jnp.int32)
    base_scr[...] = base_scr[...] + jnp.sum(multi, axis=1, keepdims=True)
    cnt_ref[...] = base_scr[...]


def _route(lgt, bias, tr):
    t = lgt.shape[1]
    blk = pl.BlockSpec((TOP_K, tr), lambda i: (0, i))
    return pl.pallas_call(
        _router_kernel,
        grid=(t // tr,),
        in_specs=[pl.BlockSpec((N_EXPERTS, tr), lambda i: (0, i)), pl.BlockSpec((N_EXPERTS, 1), lambda i: (0, 0))],
        out_specs=[blk, blk, blk, pl.BlockSpec((N_EXPERTS, 128), lambda i: (0, 0))],
        out_shape=[jax.ShapeDtypeStruct((TOP_K, t), jnp.int32), jax.ShapeDtypeStruct((TOP_K, t), F32),
                   jax.ShapeDtypeStruct((TOP_K, t), jnp.int32), jax.ShapeDtypeStruct((N_EXPERTS, 128), F32)],
        scratch_shapes=[pltpu.VMEM((N_EXPERTS, 128), F32)],
        compiler_params=_cp("arbitrary"),
    )(lgt, bias)


SC_CORES = 2
SC_SUBCORES = 16
SC_MAX_ROWS = 128


def _sc_chunk(per_worker):
    for c in range(SC_MAX_ROWS, 7, -8):
        if per_worker % c == 0:
            return c
    raise ValueError(per_worker)


def _sc_scatter_rows(rows, dest, n_slots):
    t, w = rows.shape
    top_k = dest.shape[0] // t
    workers = SC_CORES * SC_SUBCORES
    assert t % workers == 0
    per_worker = t // workers
    chunk = _sc_chunk(per_worker)
    mesh = plsc.VectorSubcoreMesh(core_axis_name="c", subcore_axis_name="s")

    @functools.partial(
        pl.kernel, mesh=mesh,
        out_type=jax.ShapeDtypeStruct((n_slots, w), rows.dtype),
        scratch_types=[pltpu.VMEM((chunk,), jnp.int32), pltpu.VMEM((chunk, w), rows.dtype),
                       pltpu.SemaphoreType.DMA],
    )
    def scatter(rows_hbm, dest_hbm, out_hbm, idx_v, rows_v, sem):
        base = (lax.axis_index("s") * SC_CORES + lax.axis_index("c")) * per_worker

        @pl.loop(0, per_worker // chunk)
        def _(ci):
            off = base + ci * chunk
            pltpu.sync_copy(rows_hbm.at[pl.ds(off, chunk)], rows_v)
            for k in range(top_k):
                pltpu.sync_copy(dest_hbm.at[pl.ds(k * t + off, chunk)], idx_v)
                pltpu.async_copy(rows_v, out_hbm.at[idx_v], sem).wait()

    return scatter(rows, dest)


def _sc_gather_rows(table, idx):
    n, w = idx.shape[0], table.shape[1]
    workers = SC_CORES * SC_SUBCORES
    assert n % workers == 0
    per_worker = n // workers
    chunk = _sc_chunk(per_worker)
    mesh = plsc.VectorSubcoreMesh(core_axis_name="c", subcore_axis_name="s")

    @functools.partial(
        pl.kernel, mesh=mesh,
        out_type=jax.ShapeDtypeStruct((n, w), table.dtype),
        scratch_types=[pltpu.VMEM((chunk,), jnp.int32), pltpu.VMEM((chunk, w), table.dtype),
                       pltpu.SemaphoreType.DMA],
    )
    def gather(table_hbm, idx_hbm, out_hbm, idx_v, rows_v, sem):
        base = (lax.axis_index("s") * SC_CORES + lax.axis_index("c")) * per_worker

        @pl.loop(0, per_worker // chunk)
        def _(ci):
            off = base + ci * chunk
            pltpu.sync_copy(idx_hbm.at[pl.ds(off, chunk)], idx_v)
            pltpu.async_copy(table_hbm.at[idx_v], rows_v, sem).wait()
            pltpu.sync_copy(rows_v, out_hbm.at[pl.ds(off, chunk)])

    return gather(table, idx)


def _ffn_kernel(be_ref, nu_ref, xs_ref, w1_ref, b1_ref, w2_ref, b2_ref, ys_ref, w1b_scr, w2b_scr):
    i = pl.program_id(0)
    new_expert = jnp.logical_or(i == 0, be_ref[i] != be_ref[jnp.maximum(i - 1, 0)])

    @pl.when(new_expert)
    def _():
        w1b_scr[...] = w1_ref[0, 0].astype(BF16)
        w2b_scr[...] = w2_ref[0, 0].astype(BF16)

    @pl.when(nu_ref[i] > 0)
    def _():
        live = lax.broadcasted_iota(jnp.int32, xs_ref.shape, 0) < nu_ref[i]
        x = _unpack_pairs(jnp.where(live, xs_ref[...], 0)).astype(BF16)
        gu = jnp.dot(x, w1b_scr[...], preferred_element_type=F32) + b1_ref[0, 0]
        g = jnp.minimum(gu[:, 0:D], SWIGLU_LIMIT)
        u = jnp.clip(gu[:, D:2 * D], -SWIGLU_LIMIT, SWIGLU_LIMIT)
        act = g * _sigmoid(SWIGLU_ALPHA * g) * (u + 1.0)
        y = jnp.dot(act.astype(BF16), w2b_scr[...], preferred_element_type=F32) + b2_ref[0, 0]
        ys_ref[...] = _pack_pairs(y)

    @pl.when(nu_ref[i] == 0)
    def _():
        ys_ref[...] = jnp.zeros_like(ys_ref)


def _expert_ffn(block_e, n_used, xs, w1, b1, w2, b2, layer):
    n_slots = xs.shape[0]
    bm = MOE_BLOCK
    grid_spec = pltpu.PrefetchScalarGridSpec(
        num_scalar_prefetch=2,
        grid=(n_slots // bm,),
        in_specs=[
            pl.BlockSpec((bm, D // 2), lambda i, be, nu: (i, 0)),
            pl.BlockSpec((1, 1, D, 2 * D), lambda i, be, nu: (layer, be[i], 0, 0)),
            pl.BlockSpec((1, 1, 1, 2 * D), lambda i, be, nu: (layer, be[i], 0, 0)),
            pl.BlockSpec((1, 1, D, D), lambda i, be, nu: (layer, be[i], 0, 0)),
            pl.BlockSpec((1, 1, 1, D), lambda i, be, nu: (layer, be[i], 0, 0)),
        ],
        out_specs=pl.BlockSpec((bm, D // 2), lambda i, be, nu: (i, 0)),
        scratch_shapes=[pltpu.VMEM((D, 2 * D), BF16), pltpu.VMEM((D, D), BF16)],
    )
    return pl.pallas_call(
        _ffn_kernel,
        grid_spec=grid_spec,
        out_shape=jax.ShapeDtypeStruct((n_slots, D // 2), jnp.int32),
        compiler_params=_cp("arbitrary"),
    )(block_e, n_used, xs, w1, b1, w2, b2)


def _combine_kernel(yg_ref, gate_ref, x_ref, g2_ref, nw_ref, o_ref):
    y = gate_ref[:, 0:1] * _unpack_pairs(yg_ref[0])
    for k in range(1, TOP_K):
        y = y + gate_ref[:, k:k + 1] * _unpack_pairs(yg_ref[k])
    o_ref[...] = x_ref[...] + g2_ref[0] * (_rms(y) * nw_ref[...])


def _combine(yg, gate_t, x1, g2_blocks, nw3, tok_off, tc):
    t = x1.shape[0]
    ob = tok_off // tc
    return pl.pallas_call(
        _combine_kernel,
        grid=(t // tc,),
        in_specs=[pl.BlockSpec((TOP_K, tc, D // 2), lambda i: (0, i + ob, 0)),
                  pl.BlockSpec((tc, TOP_K), lambda i: (i + ob, 0)),
                  pl.BlockSpec((tc, D), lambda i: (i, 0)),
                  pl.BlockSpec((1, 1, D), lambda i: (i, 0, 0)),
                  pl.BlockSpec((1, D), lambda i: (0, 0))],
        out_specs=pl.BlockSpec((tc, D), lambda i: (i, 0)),
        out_shape=jax.ShapeDtypeStruct((t, D), F32),
        compiler_params=_cp("parallel"),
    )(yg, gate_t, x1, g2_blocks, nw3)


def _rope_tables(seqlen):
    t = jnp.arange(seqlen)
    row = (t // GRID_W).astype(F32)
    col = (t % GRID_W).astype(F32)
    inv_freq = ROPE_BASE ** (-jnp.arange(ROPE_PAIRS, dtype=F32) / ROPE_PAIRS)
    ang_r = row[:, None] * inv_freq
    ang_c = col[:, None] * inv_freq
    ang = jnp.concatenate([ang_r, ang_r, ang_c, ang_c], axis=-1)
    ang = jnp.concatenate([ang, ang], axis=-1)
    first = (jnp.arange(128) % 32) < ROPE_PAIRS
    return jnp.cos(ang), jnp.where(first, -jnp.sin(ang), jnp.sin(ang))


def _repack_w_in(w):
    main = jnp.concatenate([w[:, :XBC_W], w[:, XBC_W + DT_W:]], axis=1).astype(BF16)
    wdt = jnp.pad(w[:, XBC_W:XBC_W + DT_W], ((0, 0), (0, 128 - DT_W))).astype(BF16)
    return main, wdt


def _split_dt(dt):
    b, l, _ = dt.shape
    d = dt[:, :, :DT_W].reshape(b, l // CHUNK, CHUNK, 2, SSM_HEADS)
    dc = jnp.moveaxis(d, 3, 0)
    return dc, jnp.swapaxes(dc, -1, -2)


def _moe(lgt, router_b, h2_parts, w1, b1, w2, b2, layer):
    t = lgt.shape[1]
    bm = MOE_BLOCK
    idx, gate, rank, cnt = _route(lgt, router_b.reshape(N_EXPERTS, 1), 256)
    counts = cnt[:, 0].astype(jnp.int32)
    padded = (counts + bm - 1) // bm * bm
    end_padded = jnp.cumsum(padded)
    start_padded = end_padded - padded
    experts = jnp.arange(N_EXPERTS, dtype=jnp.int32)[:, None, None]
    dest = rank + jnp.sum(jnp.where(idx[None] == experts, start_padded[:, None, None], 0), axis=0)
    n_slots = (t * TOP_K + bm - 1) // bm * bm + N_EXPERTS * bm
    n_blocks = n_slots // bm
    n_used = (end_padded[-1] // bm).astype(jnp.int32)
    blk_id = jnp.arange(n_blocks, dtype=jnp.int32)
    blk = jnp.minimum(blk_id, n_used - 1)
    block_e = jnp.sum((end_padded[None, :] <= (blk * bm)[:, None]).astype(jnp.int32), axis=1)
    block_e = jnp.minimum(block_e, N_EXPERTS - 1)
    is_e = block_e[:, None] == jnp.arange(N_EXPERTS, dtype=jnp.int32)[None, :]
    rows_left = jnp.sum(jnp.where(is_e, (start_padded + counts)[None, :], 0), axis=1) - blk_id * bm
    block_rows = jnp.where(blk_id < n_used, jnp.clip(rows_left, 0, bm), 0).astype(jnp.int32)
    h2p = h2_parts[0] if len(h2_parts) == 1 else jnp.concatenate(h2_parts, axis=0)
    xs = _sc_scatter_rows(h2p, dest.reshape(-1), n_slots)
    ys = _expert_ffn(block_e, block_rows, xs, w1, b1[:, :, None, :], w2, b2[:, :, None, :], layer)
    return dest, gate.T, ys


def kernel(x, c, ctx, c_ctx, ada_w, ada_b, norm_w, w_in, conv_w, conv_b, dt_bias, a_log, d_skip, ssm_norm_w,
           lambda_qk, subln_w, w_branch, w_out, router_w, router_b, expert_w1, expert_b1, expert_w2, expert_b2):
    bsz, seqlen, _ = x.shape
    ctx_len = ctx.shape[1]
    depth = ada_w.shape[0]
    n_tok = bsz * seqlen
    rope_cos, rope_sin = _rope_tables(seqlen)
    expand = jnp.tile(jnp.repeat(jnp.eye(SSM_HEADS, dtype=BF16), SSM_P, axis=1), (3, 1))
    tm = min(1024, seqlen)
    tq = min(2048, seqlen)

    for layer in range(depth):
        last = layer == depth - 1
        lambda_init = 0.8 - 0.6 * math.exp(-0.3 * layer)
        mod = jax.nn.silu(c) @ ada_w[layer] + ada_b[layer]
        mod_c = jax.nn.silu(c_ctx) @ ada_w[layer] + ada_b[layer]
        sh1, sc1, g1, sh2, sc2, g2 = [m[:, None, :] for m in jnp.split(mod, 6, axis=-1)]
        csh1, csc1, cg1, csh2, csc2, cg2 = [jnp.broadcast_to(m[None, None, :], (bsz, 1, D))
                                            for m in jnp.split(mod_c, 6)]
        nw = norm_w[layer]
        w_main, w_dt = _repack_w_in(w_in[layer])

        p, dt = _project(x, nw[0] * (1 + sc1), sh1, w_main, w_dt, tm)
        pc, dt_c = _project(ctx, nw[0] * (1 + csc1), csh1, w_main, w_dt, ctx_len)

        conv_wt = conv_w[layer].T
        conv_bb = conv_b[layer][None, :]
        xconv = _conv_silu(p, pc, conv_wt, conv_bb)
        dtc, dtr = _split_dt(jnp.concatenate([dt, dt_c], axis=1))
        a = -jnp.exp(a_log[layer].astype(F32))
        ys_dir = []
        for d in range(2):
            prow = jnp.stack([dt_bias[layer][d], a[d]], axis=0)
            ys_dir.append(_ssd_scan(xconv, dtc[d], dtr[d], prow, prow.T, expand, seqlen, reverse=(d == 1)))
        y_f, y_b = ys_dir

        lq = lambda_qk[layer].astype(F32)
        lam = (jnp.exp(jnp.sum(lq[0] * lq[1])) - jnp.exp(jnp.sum(lq[2] * lq[3])) + lambda_init).reshape(1)
        wn = (subln_w[layer] * (1.0 - lambda_init))[None, :]
        q_rot, k_rot, v_t = _rope(p, rope_cos, rope_sin, min(512, seqlen))
        kc, vc = COL_K // D, COL_V // D
        v_t_ctx = jnp.swapaxes(pc[:, :, COL_V:COL_V + D], 1, 2)
        y_attn = _attention_latent(lam, q_rot, k_rot, v_t, pc, kc, v_t_ctx, wn, tq, min(512, seqlen // 2))

        zr, zi = _fourier_channel(p, min(512, seqlen))
        y_four = _fourier_seq(zr, zi)

        dsk = jnp.repeat(d_skip[layer], SSM_P)[None, :]
        nsw = ssm_norm_w[layer][None, :]
        wb = w_branch[layer].astype(BF16)
        wo = w_out[layer].astype(BF16)
        rwt = router_w[layer].T
        tmm = min(256, seqlen)
        x1, h2, lgt = _merge(y_f, y_b, xconv, 0, p, y_attn, y_four, x, dsk, nsw, wb, wo, nw[1][None, :],
                             g1, nw[2] * (1 + sc2), sh2, rwt, tmm)
        h2_parts = [h2.reshape(n_tok, D // 2)]
        lgt = jnp.moveaxis(lgt, 0, 1).reshape(N_EXPERTS, n_tok)

        if not last:
            y_attn_c = _attention(lam, pc, COL_Q // D, [(pc, kc, pc, vc)], wn, ctx_len, ctx_len)
            zr_c, zi_c = _fourier_channel(pc, ctx_len)
            y_four_c = _fourier_seq_direct(zr_c, zi_c)
            ctx1, hc2, lgt_c = _merge(y_f, y_b, xconv, seqlen // ctx_len, pc, y_attn_c, y_four_c, ctx, dsk, nsw,
                                      wb, wo, nw[1][None, :], cg1, nw[2] * (1 + csc2), csh2, rwt, ctx_len)
            h2_parts.append(hc2.reshape(bsz * ctx_len, D // 2))
            lgt = jnp.concatenate([lgt, jnp.moveaxis(lgt_c, 0, 1).reshape(N_EXPERTS, bsz * ctx_len)], axis=1)

        dest, gate_t, ys = _moe(lgt, router_b[layer], h2_parts, expert_w1, expert_b1, expert_w2, expert_b2, layer)
        n_routed = dest.shape[1]
        yg = _sc_gather_rows(ys, dest.reshape(-1)).reshape(TOP_K, n_routed, D // 2)
        tc = 256
        g2_blocks = jnp.repeat(g2, seqlen // tc, axis=0)
        x = _combine(yg, gate_t, x1.reshape(n_tok, D), g2_blocks, nw[3][None, :], 0, tc).reshape(x.shape)
        if not last:
            cg2_blocks = jnp.repeat(cg2, ctx_len // tc, axis=0)
            ctx = _combine(yg, gate_t, ctx1.reshape(bsz * ctx_len, D), cg2_blocks, nw[3][None, :],
                           n_tok, tc).reshape(ctx.shape)
    return x
```
